```python
import math
import jax, jax.numpy as jnp
from jax import lax
import numpy as np

D_MODEL = 1024
BATCH = 4
SEQ = 4096
DEPTH = 4

HEAD_DIM = 64
ATTN_SCALE = HEAD_DIM ** -0.5
SWA_HEADS = 8
SWA_KV_HEADS = 2
SWA_WINDOW = 128
SWA_Q = SWA_HEADS * HEAD_DIM
SWA_KV = SWA_KV_HEADS * HEAD_DIM
SSM_WIDTH = 512
SSM_GROUP = 16
SSM_GROUPS = SSM_WIDTH // SSM_GROUP
SSM_STATE = 64
DT_MIN = 0.001
DT_MAX = 0.1
MOBA_HEADS = 8
MOBA_KV_HEADS = 2
MOBA_BLOCK = 256
MOBA_TOPK = 3
MOBA_Q_CHUNK = 64
MOBA_Q = MOBA_HEADS * HEAD_DIM
MOBA_KV = MOBA_KV_HEADS * HEAD_DIM
N_BRANCH = 3
D_FF = 4 * D_MODEL
NORM_EPS = 1e-6
ALIBI_MAX_BIAS = 8.0
IN_SIZES = (SWA_Q, SWA_KV, SWA_KV, SSM_WIDTH, MOBA_Q, MOBA_KV, MOBA_KV, N_BRANCH * D_MODEL)
D_IN = sum(IN_SIZES)

kernel_name = "hybrid_swa_s5_moba_gated"


def rms_norm(x, gain):
    xf = x.astype(jnp.float32)
    y = xf * lax.rsqrt(jnp.mean(xf * xf, axis=-1, keepdims=True) + NORM_EPS)
    return (y * gain.astype(jnp.float32)).astype(x.dtype)


def alibi_slopes():
    n = SWA_HEADS + MOBA_HEADS
    return jnp.asarray(2.0 ** (-ALIBI_MAX_BIAS * np.arange(1, n + 1) / n), jnp.float32)


def split_in(proj):
    outs, start = [], 0
    for size in IN_SIZES:
        outs.append(proj[..., start:start + size])
        start += size
    return outs


def swa_attention(q, k, v, sinks, slopes):
    B, S, Hq, d = q.shape
    Hkv = k.shape[2]
    G = Hq // Hkv
    W = SWA_WINDOW
    nb = S // W
    qb = q.reshape(B, nb, W, Hkv, G, d)
    kp = jnp.pad(k, ((0, 0), (W, 0), (0, 0), (0, 0))).reshape(B, nb + 1, W, Hkv, d)
    vp = jnp.pad(v, ((0, 0), (W, 0), (0, 0), (0, 0))).reshape(B, nb + 1, W, Hkv, d)
    kb = jnp.concatenate([kp[:, :-1], kp[:, 1:]], axis=2)
    vb = jnp.concatenate([vp[:, :-1], vp[:, 1:]], axis=2)
    s = jnp.einsum('bnqhgd,bnkhd->bnhgqk', qb, kb).astype(jnp.float32) * ATTN_SCALE
    t_rel = jnp.arange(W)[:, None]
    s_rel = jnp.arange(2 * W)[None, :] - W
    dist = t_rel - s_rel
    band = (dist >= 0) & (dist < W)
    mask = band[None] & ((jnp.arange(nb)[:, None, None] * W + s_rel[None]) >= 0)
    bias = -slopes.reshape(Hkv, G)[:, :, None, None] * dist.astype(jnp.float32)[None, None]
    s = jnp.where(mask[None, :, None, None], s + bias[None, None], -jnp.inf)
    sink = jnp.broadcast_to(sinks.astype(jnp.float32).reshape(Hkv, G)[None, None, :, :, None, None],
                            s.shape[:-1] + (1,))
    p = jax.nn.softmax(jnp.concatenate([s, sink], axis=-1), axis=-1)[..., :-1].astype(v.dtype)
    o = jnp.einsum('bnhgqk,bnkhd->bnqhgd', p, vb)
    return o.reshape(B, S, Hq * d)


def moba_attention(q, k, v, slopes):
    B, S, Hq, d = q.shape
    Hkv = k.shape[2]
    G = Hq // Hkv
    L = MOBA_BLOCK
    Qc = MOBA_Q_CHUNK
    Sp = -(-S // L) * L
    nblk = Sp // L
    topk = min(MOBA_TOPK, nblk)
    pad = ((0, 0), (0, Sp - S), (0, 0), (0, 0))
    q, k, v = jnp.pad(q, pad), jnp.pad(k, pad), jnp.pad(v, pad)
    k_blocks = jnp.repeat(k, G, axis=2).transpose(0, 2, 1, 3).reshape(B, Hq, nblk, L, d)
    v_blocks = jnp.repeat(v, G, axis=2).transpose(0, 2, 1, 3).reshape(B, Hq, nblk, L, d)
    k_mean = jnp.mean(k_blocks.astype(jnp.float32), axis=3)
    nq = Sp // Qc
    q_chunks = q.transpose(0, 2, 1, 3).reshape(B, Hq, nq, Qc, d).transpose(2, 0, 1, 3, 4)
    b_idx = jnp.arange(B)[:, None, None, None]
    h_idx = jnp.arange(Hq)[None, :, None, None]
    blk_ids = jnp.arange(nblk)
    rank_ids = jnp.arange(topk)
    offs = jnp.arange(L)

    def one_chunk(args):
        qi, ci = args
        t = ci * Qc + jnp.arange(Qc)
        j = (ci * Qc) // L
        gate = jnp.einsum('bhqd,bhnd->bhqn', qi.astype(jnp.float32), k_mean)
        gate = jnp.where(blk_ids < j, gate, -jnp.inf)
        _, sel = lax.top_k(gate, topk)
        valid = rank_ids < j
        k_sel = k_blocks[b_idx, h_idx, sel]
        v_sel = v_blocks[b_idx, h_idx, sel]
        s_sel = jnp.einsum('bhqd,bhqrkd->bhqrk', qi, k_sel).astype(jnp.float32) * ATTN_SCALE
        dist_sel = (t[None, None, :, None, None] - (sel[..., None] * L + offs)).astype(jnp.float32)
        s_sel = s_sel - slopes[None, :, None, None, None] * dist_sel
        s_sel = jnp.where(valid[None, None, None, :, None], s_sel, -jnp.inf)
        k_own = lax.dynamic_slice_in_dim(k_blocks, j, 1, axis=2)[:, :, 0]
        v_own = lax.dynamic_slice_in_dim(v_blocks, j, 1, axis=2)[:, :, 0]
        s_own = jnp.einsum('bhqd,bhkd->bhqk', qi, k_own).astype(jnp.float32) * ATTN_SCALE
        dist_own = t[:, None] - (j * L + offs)[None, :]
        s_own = s_own - slopes[None, :, None, None] * dist_own.astype(jnp.float32)[None, None]
        s_own = jnp.where((dist_own >= 0)[None, None], s_own, -jnp.inf)
        logits = jnp.concatenate([s_sel.reshape(B, Hq, Qc, topk * L), s_own], axis=-1)
        p = jax.nn.softmax(logits, axis=-1).astype(v_blocks.dtype)
        p_sel = p[..., :topk * L].reshape(B, Hq, Qc, topk, L)
        o = (jnp.einsum('bhqrk,bhqrkd->bhqd', p_sel, v_sel)
             + jnp.einsum('bhqk,bhkd->bhqd', p[..., topk * L:], v_own))
        return o

    o = lax.map(one_chunk, (q_chunks, jnp.arange(nq)))
    o = o.transpose(1, 0, 3, 2, 4).reshape(B, Sp, Hq * d)
    return o[:, :S]


def s5_ssm(u, lam_re, lam_im, log_step, b_re, b_im, c_re, c_im, d_skip):
    Bsz, S, _ = u.shape
    f32 = jnp.float32
    uf = u.astype(f32).reshape(Bsz, S, SSM_GROUPS, SSM_GROUP)
    lr, li = lam_re.astype(f32), lam_im.astype(f32)
    dt = jnp.exp(log_step.astype(f32))[:, None]
    mag = jnp.exp(lr * dt)
    ab_re, ab_im = mag * jnp.cos(li * dt), mag * jnp.sin(li * dt)
    den = lr * lr + li * li
    nr, ni = ab_re - 1.0, ab_im
    f_re, f_im = (nr * lr + ni * li) / den, (ni * lr - nr * li) / den
    br, bi = b_re.astype(f32), b_im.astype(f32)
    bb_re = f_re[..., None] * br - f_im[..., None] * bi
    bb_im = f_re[..., None] * bi + f_im[..., None] * br
    x_re = jnp.einsum('bsgh,gph->bsgp', uf, bb_re)
    x_im = jnp.einsum('bsgh,gph->bsgp', uf, bb_im)
    a_re = jnp.broadcast_to(ab_re, x_re.shape)
    a_im = jnp.broadcast_to(ab_im, x_im.shape)

    def combine(e1, e2):
        a1r, a1i, b1r, b1i = e1
        a2r, a2i, b2r, b2i = e2
        return (a2r * a1r - a2i * a1i, a2r * a1i + a2i * a1r,
                a2r * b1r - a2i * b1i + b2r, a2r * b1i + a2i * b1r + b2i)

    _, _, h_re, h_im = lax.associative_scan(combine, (a_re, a_im, x_re, x_im), axis=1)
    y = (jnp.einsum('bsgp,ghp->bsgh', h_re, c_re.astype(f32))
         - jnp.einsum('bsgp,ghp->bsgh', h_im, c_im.astype(f32))
         + d_skip.astype(f32).reshape(SSM_GROUPS, SSM_GROUP) * uf)
    return y.reshape(Bsz, S, SSM_WIDTH)


def setup_inputs(seed: int = 0) -> dict:
    key = jax.random.key(seed)
    ks = jax.random.split(key, 24)
    f32 = jnp.float32
    nrm = lambda k, shape, scale: jax.random.normal(k, shape, f32) * scale
    x = jax.random.normal(ks[0], (BATCH, SEQ, D_MODEL), f32)
    norm_mix = 1.0 + nrm(ks[1], (DEPTH, D_MODEL), 0.02)
    w_in = nrm(ks[2], (DEPTH, D_MODEL, D_IN), D_MODEL ** -0.5)
    sinks = nrm(ks[3], (DEPTH, SWA_HEADS), 0.5)
    lam_re = -0.5 + nrm(ks[4], (DEPTH, SSM_GROUPS, SSM_STATE), 0.01)
    lam_im = (math.pi * jnp.arange(SSM_STATE, dtype=f32))[None, None, :] + nrm(ks[5], (DEPTH, SSM_GROUPS, SSM_STATE), 0.01)
    log_step = math.log(DT_MIN) + jax.random.uniform(ks[6], (DEPTH, SSM_GROUPS), f32) * (math.log(DT_MAX) - math.log(DT_MIN))
    b_re = nrm(ks[7], (DEPTH, SSM_GROUPS, SSM_STATE, SSM_GROUP), (2 * SSM_GROUP) ** -0.5)
    b_im = nrm(ks[8], (DEPTH, SSM_GROUPS, SSM_STATE, SSM_GROUP), (2 * SSM_GROUP) ** -0.5)
    c_re = nrm(ks[9], (DEPTH, SSM_GROUPS, SSM_GROUP, SSM_STATE), (2 * SSM_STATE) ** -0.5)
    c_im = nrm(ks[10], (DEPTH, SSM_GROUPS, SSM_GROUP, SSM_STATE), (2 * SSM_STATE) ** -0.5)
    d_skip = nrm(ks[11], (DEPTH, SSM_WIDTH), 1.0)
    w_glu = nrm(ks[12], (DEPTH, SSM_WIDTH, 2 * SSM_WIDTH), SSM_WIDTH ** -0.5)
    w_o_swa = nrm(ks[13], (DEPTH, SWA_Q, D_MODEL), SWA_Q ** -0.5)
    w_o_ssm = nrm(ks[14], (DEPTH, SSM_WIDTH, D_MODEL), SSM_WIDTH ** -0.5)
    w_o_moba = nrm(ks[15], (DEPTH, MOBA_Q, D_MODEL), MOBA_Q ** -0.5)
    w_out = nrm(ks[16], (DEPTH, D_MODEL, D_MODEL), D_MODEL ** -0.5)
    norm_ffn = 1.0 + nrm(ks[17], (DEPTH, D_MODEL), 0.02)
    w_ff1 = nrm(ks[18], (DEPTH, D_MODEL, D_FF), D_MODEL ** -0.5)
    w_ff2 = nrm(ks[19], (DEPTH, D_FF, D_MODEL), D_FF ** -0.5)
    norm_final = 1.0 + nrm(ks[20], (D_MODEL,), 0.02)
    return {"x": x, "norm_mix": norm_mix, "w_in": w_in, "sinks": sinks,
            "lam_re": lam_re, "lam_im": lam_im, "log_step": log_step,
            "b_re": b_re, "b_im": b_im, "c_re": c_re, "c_im": c_im, "d_skip": d_skip,
            "w_glu": w_glu, "w_o_swa": w_o_swa, "w_o_ssm": w_o_ssm, "w_o_moba": w_o_moba,
            "w_out": w_out, "norm_ffn": norm_ffn, "w_ff1": w_ff1, "w_ff2": w_ff2,
            "norm_final": norm_final}


def reference(x, norm_mix, w_in, sinks, lam_re, lam_im, log_step, b_re, b_im, c_re, c_im,
              d_skip, w_glu, w_o_swa, w_o_ssm, w_o_moba, w_out, norm_ffn, w_ff1, w_ff2,
              norm_final):
    B, S, _ = x.shape
    slopes = alibi_slopes()
    slopes_swa, slopes_moba = slopes[:SWA_HEADS], slopes[SWA_HEADS:]
    for l in range(DEPTH):
        h = rms_norm(x, norm_mix[l])
        qa, ka, va, u, qm, km, vm, gates = split_in(h @ w_in[l])
        oa = swa_attention(qa.reshape(B, S, SWA_HEADS, HEAD_DIM),
                           ka.reshape(B, S, SWA_KV_HEADS, HEAD_DIM),
                           va.reshape(B, S, SWA_KV_HEADS, HEAD_DIM), sinks[l], slopes_swa)
        ya = oa @ w_o_swa[l]
        ys = s5_ssm(u, lam_re[l], lam_im[l], log_step[l], b_re[l], b_im[l], c_re[l], c_im[l], d_skip[l])
        z = jax.nn.gelu(ys).astype(x.dtype) @ w_glu[l]
        yb = (z[..., :SSM_WIDTH] * jax.nn.sigmoid(z[..., SSM_WIDTH:])) @ w_o_ssm[l]
        om = moba_attention(qm.reshape(B, S, MOBA_HEADS, HEAD_DIM),
                            km.reshape(B, S, MOBA_KV_HEADS, HEAD_DIM),
                            vm.reshape(B, S, MOBA_KV_HEADS, HEAD_DIM), slopes_moba)
        yc = om @ w_o_moba[l]
        g = jax.nn.sigmoid(gates)
        mixed = (g[..., :D_MODEL] * ya + g[..., D_MODEL:2 * D_MODEL] * yb
                 + g[..., 2 * D_MODEL:] * yc)
        x = x + mixed @ w_out[l]
        h = rms_norm(x, norm_ffn[l])
        x = x + jnp.square(jax.nn.relu(h @ w_ff1[l])) @ w_ff2[l]
    return rms_norm(x, norm_final)
```

```python
import functools
import math

import numpy as np
import jax
import jax.numpy as jnp
from jax import lax
from jax.experimental import pallas as pl
from jax.experimental.pallas import tpu as pltpu

F32 = jnp.float32
BF16 = jnp.bfloat16

HEAD_DIM = 64
N_HEADS = 8
KV_GROUP = 4
PAIR_ORDER = (0, 4, 1, 5, 2, 6, 3, 7)
ATTN_SCALE = HEAD_DIM ** -0.5
SWA_WINDOW = 128
MOBA_BLOCK = 256
MOBA_TOPK = 3
SSM_GROUPS = 32
SSM_GROUP = 16
SSM_STATE = 64
SSM_WIDTH = SSM_GROUPS * SSM_GROUP
SSM_LANES = SSM_GROUPS * SSM_STATE
NORM_EPS = 1e-6
ALIBI_MAX_BIAS = 8.0
MASK_VALUE = -1e30

LANES = 128
SUBLANES = 8
VMEM_LIMIT = 56 * 1024 * 1024

ROW_TILE = 512
SSM_CHUNK = 128
SWA_TILE = 512
SCAN_LANES = 512

_NT = (((1,), (1,)), ((), ()))


def _params(*sem):
    return pltpu.CompilerParams(dimension_semantics=sem, vmem_limit_bytes=VMEM_LIMIT)


def _resident(shape):
    zeros = (0,) * len(shape)
    return pl.BlockSpec(shape, lambda *_: zeros)


def _rms(x, gain):
    return x * lax.rsqrt(jnp.mean(x * x, axis=-1, keepdims=True) + NORM_EPS) * gain


def _split3(x):
    hi = x.astype(BF16).astype(F32)
    mid = (x - hi).astype(BF16).astype(F32)
    lo = (x - hi - mid).astype(BF16).astype(F32)
    return hi, mid, lo


_PROJ_SIZES = (512, 128, 128, 512, 512, 128, 128)


def _norm_proj_kernel(x_ref, g_ref, w_ref, qa_ref, ka_ref, va_ref, u_ref, qm_ref, km_ref, vm_ref):
    h = _rms(x_ref[...], g_ref[...]).astype(BF16)
    proj = jnp.dot(h, w_ref[...], preferred_element_type=F32)
    start = 0
    for ref, size in zip((qa_ref, ka_ref, va_ref, u_ref, qm_ref, km_ref, vm_ref), _PROJ_SIZES):
        ref[...] = proj[:, start:start + size].astype(ref.dtype)
        start += size


def _norm_proj(x, gain, w):
    rows, d = x.shape
    dts = (BF16, BF16, BF16, F32, BF16, BF16, BF16)
    return pl.pallas_call(
        _norm_proj_kernel,
        grid=(rows // ROW_TILE,),
        in_specs=[pl.BlockSpec((ROW_TILE, d), lambda i: (i, 0)), _resident(gain.shape), _resident(w.shape)],
        out_specs=[pl.BlockSpec((ROW_TILE, n), lambda i: (i, 0)) for n in _PROJ_SIZES],
        out_shape=[jax.ShapeDtypeStruct((rows, n), dt) for n, dt in zip(_PROJ_SIZES, dts)],
        compiler_params=_params("parallel"),
        name="norm_proj",
    )(x, gain, w)


def _swa_kernel(q_ref, k_ref, v_ref, e_ref, aug_ref, sink_ref, o_ref):
    qi = pl.program_id(1)
    w = SWA_WINDOW
    lane = lax.broadcasted_iota(jnp.int32, (1, LANES), 1)
    low = lane < HEAD_DIM
    row = lax.broadcasted_iota(jnp.int32, (w, 2 * w), 0)
    col = lax.broadcasted_iota(jnp.int32, (w, 2 * w), 1)
    band = (col > row) & (col <= row + w)
    e_cols = e_ref[...]
    for blk in range(SWA_TILE // w):
        t0 = qi * SWA_TILE + blk * w
        prev = pl.multiple_of(jnp.maximum(t0 - w, 0), w)
        cur = pl.multiple_of(t0, w)
        kwin = jnp.concatenate([k_ref[pl.ds(prev, w), :], k_ref[pl.ds(cur, w), :]], axis=0)
        vwin = jnp.concatenate([v_ref[pl.ds(prev, w), :], v_ref[pl.ds(cur, w), :]], axis=0)
        mask = band & ((col >= w) | (t0 > 0))
        for pair in range(N_HEADS // 2):
            qp = q_ref[blk * w:(blk + 1) * w, pair * LANES:(pair + 1) * LANES]
            pv = None
            inv = None
            for half in range(2):
                head = pair + KV_GROUP * half
                hm = low if half == 0 else jnp.logical_not(low)
                qcat = jnp.concatenate([qp, aug_ref[head]], axis=1)
                kcat = jnp.concatenate([jnp.where(hm, kwin, jnp.zeros_like(kwin)), e_cols], axis=1)
                s = lax.dot_general(qcat, kcat, _NT, preferred_element_type=F32)
                s = jnp.where(mask, s, MASK_VALUE)
                sink = sink_ref[head:head + 1, 0:1]
                m = jnp.maximum(jnp.max(s, axis=1, keepdims=True), sink)
                p = jnp.exp(s - m)
                den = jnp.sum(p, axis=1, keepdims=True) + jnp.exp(sink - m)
                o = jnp.dot(p.astype(BF16), jnp.where(hm, vwin, jnp.zeros_like(vwin)),
                            preferred_element_type=F32)
                r = 1.0 / den
                pv = o if pv is None else pv + o
                inv = r if inv is None else jnp.where(low, inv, r)
            o_ref[blk * w:(blk + 1) * w, pair * LANES:(pair + 1) * LANES] = (pv * inv).astype(BF16)


def _swa(q, k, v, e_cols, aug, sinks, batch):
    seq = q.shape[0]
    qw = N_HEADS * HEAD_DIM
    kw = k.shape[1] // batch
    return pl.pallas_call(
        _swa_kernel,
        grid=(batch, seq // SWA_TILE),
        in_specs=[pl.BlockSpec((SWA_TILE, qw), lambda b, i: (i, b)),
                  pl.BlockSpec((seq, kw), lambda b, i: (0, b)),
                  pl.BlockSpec((seq, kw), lambda b, i: (0, b)),
                  _resident(e_cols.shape), _resident(aug.shape), _resident(sinks.shape)],
        out_specs=pl.BlockSpec((SWA_TILE, qw), lambda b, i: (i, b)),
        out_shape=jax.ShapeDtypeStruct(q.shape, BF16),
        compiler_params=_params("parallel", "arbitrary"),
        name="swa_attention",
    )(q, k, v, e_cols, aug, sinks)


def _moba_kernel(q_ref, k_ref, v_ref, e_ref, aug_ref, o_ref, kmean_ref, *, nblk):
    j = pl.program_id(1)
    L = MOBA_BLOCK

    @pl.when(j == 0)
    def _():
        kf = k_ref[...].astype(F32)
        kmean_ref[...] = jnp.zeros(kmean_ref.shape, F32)
        kmean_ref[0:nblk, :] = jnp.mean(kf.reshape(nblk, L, LANES), axis=1)

    lane = lax.broadcasted_iota(jnp.int32, (1, LANES), 1)
    low = lane < HEAD_DIM
    high = jnp.logical_not(low)
    kmean = kmean_ref[...]
    blk = lax.broadcasted_iota(jnp.int32, (2 * SUBLANES, L), 0)
    row = lax.broadcasted_iota(jnp.int32, (L, L), 0)
    col = lax.broadcasted_iota(jnp.int32, (L, L), 1)
    causal = col <= row
    own = pl.multiple_of(j * L, L)

    def scores(off, qcat, hm):
        kb = k_ref[pl.ds(off, L), :]
        kcat = jnp.concatenate([jnp.where(hm, kb, jnp.zeros_like(kb)), e_ref[pl.ds(off, L), :]], axis=1)
        return lax.dot_general(qcat, kcat, _NT, preferred_element_type=F32)

    def values(off, p_lo, p_hi):
        vb = v_ref[pl.ds(off, L), :]
        return (jnp.dot(p_lo.astype(BF16), jnp.where(low, vb, jnp.zeros_like(vb)), preferred_element_type=F32)
                + jnp.dot(p_hi.astype(BF16), jnp.where(high, vb, jnp.zeros_like(vb)), preferred_element_type=F32))

    for pair in range(N_HEADS // 2):
        qp = q_ref[:, pair * LANES:(pair + 1) * LANES]
        qf = qp.astype(F32)
        qcats = []
        for half in range(2):
            head = pair + KV_GROUP * half
            hm = low if half == 0 else high
            gate = lax.dot_general(jnp.where(hm, kmean, 0.0), qf, _NT, precision=lax.Precision.HIGHEST,
                                   preferred_element_type=F32)[0:2 * SUBLANES, :]
            rank = jnp.zeros(gate.shape, jnp.int32)
            for m in range(nblk):
                gm = gate[m:m + 1, :]
                beats = (gm > gate) | ((gm == gate) & (m < blk))
                rank = rank + jnp.where(beats & (m < j), 1, 0)
            selected = ((blk < j) & (rank < MOBA_TOPK)) | (blk == j)
            bias_t = jnp.where(selected, 0.0, MASK_VALUE)
            bias = jnp.concatenate([bias_t, jnp.zeros((LANES - 2 * SUBLANES, L), F32)], axis=0).T
            aug = jnp.where(lane < 2 * SUBLANES, bias, aug_ref[head:head + 1, :]).astype(BF16)
            qcats.append(jnp.concatenate([qp, aug], axis=1))
        q_lo, q_hi = qcats

        s_lo = jnp.where(causal, scores(own, q_lo, low), MASK_VALUE)
        s_hi = jnp.where(causal, scores(own, q_hi, high), MASK_VALUE)
        m_lo = jnp.max(s_lo, axis=1, keepdims=True)
        m_hi = jnp.max(s_hi, axis=1, keepdims=True)
        p_lo = jnp.exp(s_lo - m_lo)
        p_hi = jnp.exp(s_hi - m_hi)
        l_lo = jnp.sum(p_lo, axis=1, keepdims=True)
        l_hi = jnp.sum(p_hi, axis=1, keepdims=True)
        acc = values(own, p_lo, p_hi)

        def body(n, carry, q_lo=q_lo, q_hi=q_hi):
            m_lo, l_lo, m_hi, l_hi, acc = carry
            off = pl.multiple_of(n * L, L)
            s_lo = scores(off, q_lo, low)
            s_hi = scores(off, q_hi, high)
            mn_lo = jnp.maximum(m_lo, jnp.max(s_lo, axis=1, keepdims=True))
            mn_hi = jnp.maximum(m_hi, jnp.max(s_hi, axis=1, keepdims=True))
            a_lo = jnp.exp(m_lo - mn_lo)
            a_hi = jnp.exp(m_hi - mn_hi)
            p_lo = jnp.exp(s_lo - mn_lo)
            p_hi = jnp.exp(s_hi - mn_hi)
            l_lo = a_lo * l_lo + jnp.sum(p_lo, axis=1, keepdims=True)
            l_hi = a_hi * l_hi + jnp.sum(p_hi, axis=1, keepdims=True)
            acc = jnp.where(low, a_lo, a_hi) * acc + values(off, p_lo, p_hi)
            return mn_lo, l_lo, mn_hi, l_hi, acc

        m_lo, l_lo, m_hi, l_hi, acc = lax.fori_loop(0, j, body, (m_lo, l_lo, m_hi, l_hi, acc))
        o_ref[:, pair * LANES:(pair + 1) * LANES] = (acc / jnp.where(low, l_lo, l_hi)).astype(BF16)


def _moba(q, k, v, e_cols, aug, batch):
    seq = q.shape[0]
    nblk = seq // MOBA_BLOCK
    qw = N_HEADS * HEAD_DIM
    kw = k.shape[1] // batch
    return pl.pallas_call(
        functools.partial(_moba_kernel, nblk=nblk),
        grid=(batch, nblk),
        in_specs=[pl.BlockSpec((MOBA_BLOCK, qw), lambda b, i: (i, b)),
                  pl.BlockSpec((seq, kw), lambda b, i: (0, b)),
                  pl.BlockSpec((seq, kw), lambda b, i: (0, b)),
                  _resident(e_cols.shape), _resident(aug.shape)],
        out_specs=pl.BlockSpec((MOBA_BLOCK, qw), lambda b, i: (i, b)),
        out_shape=jax.ShapeDtypeStruct(q.shape, BF16),
        scratch_shapes=[pltpu.VMEM((LANES, LANES), F32)],
        compiler_params=_params("parallel", "arbitrary"),
        name="moba_attention",
    )(q, k, v, e_cols, aug)


def _ssm_kernel(u_ref, wbr_ref, wbi_ref, a_ref, cr_ref, ci_ref, d_ref, wglu_ref, o_ref,
                xr_ref, xi_ref, hr_ref, hi_ref, *, batch):
    rows = u_ref.shape[0]
    half_lanes = SSM_LANES // 2
    half_width = SSM_WIDTH // 2

    @pl.when(pl.program_id(0) == 0)
    def _():
        hr_ref[...] = jnp.zeros(hr_ref.shape, F32)
        hi_ref[...] = jnp.zeros(hi_ref.shape, F32)

    u = u_ref[...]
    ub = u.astype(BF16)
    for s in range(2):
        us = ub[:, s * half_width:(s + 1) * half_width]
        xr_ref[:, s * half_lanes:(s + 1) * half_lanes] = jnp.dot(us, wbr_ref[s], preferred_element_type=F32)
        xi_ref[:, s * half_lanes:(s + 1) * half_lanes] = jnp.dot(us, wbi_ref[s], preferred_element_type=F32)

    first_step = lax.broadcasted_iota(jnp.int32, (SUBLANES, SCAN_LANES), 0) < batch
    for c in range(SSM_LANES // SCAN_LANES):
        lanes = slice(c * SCAN_LANES, (c + 1) * SCAN_LANES)
        ar = jnp.broadcast_to(a_ref[0:1, lanes], (SUBLANES, SCAN_LANES))
        ai = jnp.broadcast_to(a_ref[1:2, lanes], (SUBLANES, SCAN_LANES))

        def body(k, carry, lanes=lanes, ar=ar, ai=ai):
            pr, pi = carry
            rws = pl.ds(pl.multiple_of(k * SUBLANES, SUBLANES), SUBLANES)
            xr = xr_ref[rws, lanes]
            xi = xi_ref[rws, lanes]
            h1r = ar * pr - ai * pi + xr
            h1i = ar * pi + ai * pr + xi
            s1r = pltpu.roll(h1r, batch, axis=0)
            s1i = pltpu.roll(h1i, batch, axis=0)
            h2r = ar * s1r - ai * s1i + xr
            h2i = ar * s1i + ai * s1r + xi
            xr_ref[rws, lanes] = jnp.where(first_step, h1r, h2r)
            xi_ref[rws, lanes] = jnp.where(first_step, h1i, h2i)
            return pltpu.roll(h2r, batch, axis=0), pltpu.roll(h2i, batch, axis=0)

        pr, pi = lax.fori_loop(0, rows // SUBLANES, body, (hr_ref[:, lanes], hi_ref[:, lanes]))
        hr_ref[:, lanes] = pr
        hi_ref[:, lanes] = pi

    ys = []
    for s in range(2):
        hr = xr_ref[:, s * half_lanes:(s + 1) * half_lanes].astype(BF16)
        hi = xi_ref[:, s * half_lanes:(s + 1) * half_lanes].astype(BF16)
        ys.append(jnp.dot(hr, cr_ref[s], preferred_element_type=F32)
                  + jnp.dot(hi, ci_ref[s], preferred_element_type=F32))
    y = jnp.concatenate(ys, axis=1) + d_ref[...] * u
    z = jnp.dot(jax.nn.gelu(y).astype(BF16), wglu_ref[...], preferred_element_type=F32)
    o_ref[...] = (z[:, :SSM_WIDTH] * jax.nn.sigmoid(z[:, SSM_WIDTH:])).astype(BF16)


def _ssm(u, wbr, wbi, a, cr, ci, d, wglu, batch):
    rows = u.shape[0]
    tile = SSM_CHUNK * batch
    return pl.pallas_call(
        functools.partial(_ssm_kernel, batch=batch),
        grid=(rows // tile,),
        in_specs=[pl.BlockSpec((tile, SSM_WIDTH), lambda i: (i, 0))]
                 + [_resident(t.shape) for t in (wbr, wbi, a, cr, ci, d, wglu)],
        out_specs=pl.BlockSpec((tile, SSM_WIDTH), lambda i: (i, 0)),
        out_shape=jax.ShapeDtypeStruct((rows, SSM_WIDTH), BF16),
        scratch_shapes=[pltpu.VMEM((tile, SSM_LANES), F32), pltpu.VMEM((tile, SSM_LANES), F32),
                        pltpu.VMEM((SUBLANES, SSM_LANES), F32), pltpu.VMEM((SUBLANES, SSM_LANES), F32)],
        compiler_params=_params("arbitrary"),
        name="s5_ssm",
    )(u, wbr, wbi, a, cr, ci, d, wglu)


def _merge_kernel(x_ref, oa_ref, gl_ref, om_ref, g_ref, wg_ref, woa_ref, wos_ref, wom_ref, wout_ref, o_ref):
    x = x_ref[...]
    d = x.shape[1]
    h = _rms(x, g_ref[...]).astype(BF16)
    mixed = None
    for n, (b_ref, w_ref) in enumerate(((oa_ref, woa_ref), (gl_ref, wos_ref), (om_ref, wom_ref))):
        gate = jax.nn.sigmoid(jnp.dot(h, wg_ref[:, n * d:(n + 1) * d], preferred_element_type=F32))
        term = gate * jnp.dot(b_ref[...], w_ref[...], preferred_element_type=F32)
        mixed = term if mixed is None else mixed + term
    o_ref[...] = x + jnp.dot(mixed.astype(BF16), wout_ref[...], preferred_element_type=F32)


def _merge(x, oa, gl, om, gain, wg, woa, wos, wom, wout):
    rows, d = x.shape
    tile = lambda n: pl.BlockSpec((ROW_TILE, n), lambda i: (i, 0))
    return pl.pallas_call(
        _merge_kernel,
        grid=(rows // ROW_TILE,),
        in_specs=[tile(d), tile(oa.shape[1]), tile(gl.shape[1]), tile(om.shape[1])]
                 + [_resident(t.shape) for t in (gain, wg, woa, wos, wom, wout)],
        out_specs=tile(d),
        out_shape=jax.ShapeDtypeStruct((rows, d), F32),
        compiler_params=_params("parallel"),
        name="merge",
    )(x, oa, gl, om, gain, wg, woa, wos, wom, wout)


def _ffn_kernel(x_ref, g_ref, w1_ref, w2_ref, gf_ref, o_ref, *, final_norm):
    x = x_ref[...]
    d = x.shape[1]
    h = _rms(x, g_ref[...]).astype(BF16)
    acc = x
    for c in range(w1_ref.shape[1] // d):
        a = jnp.maximum(jnp.dot(h, w1_ref[:, c * d:(c + 1) * d], preferred_element_type=F32), 0.0)
        acc = acc + jnp.dot((a * a).astype(BF16), w2_ref[c * d:(c + 1) * d, :], preferred_element_type=F32)
    o_ref[...] = _rms(acc, gf_ref[...]) if final_norm else acc


def _ffn(x, gain, w1, w2, gain_final, final_norm):
    rows, d = x.shape
    return pl.pallas_call(
        functools.partial(_ffn_kernel, final_norm=final_norm),
        grid=(rows // ROW_TILE,),
        in_specs=[pl.BlockSpec((ROW_TILE, d), lambda i: (i, 0))]
                 + [_resident(t.shape) for t in (gain, w1, w2, gain_final)],
        out_specs=pl.BlockSpec((ROW_TILE, d), lambda i: (i, 0)),
        out_shape=jax.ShapeDtypeStruct((rows, d), F32),
        compiler_params=_params("parallel"),
        name="ffn",
    )(x, gain, w1, w2, gain_final)


def _alibi_slopes():
    n = 2 * N_HEADS
    return jnp.asarray(2.0 ** (-ALIBI_MAX_BIAS * np.arange(1, n + 1) / n), F32)


def _pair_cols(w):
    d = w.shape[0]
    return w.reshape(d, N_HEADS, HEAD_DIM)[:, PAIR_ORDER, :].reshape(d, N_HEADS * HEAD_DIM)


def _pair_rows(w):
    n = w.shape[1]
    return w.reshape(N_HEADS, HEAD_DIM, n)[PAIR_ORDER, :, :].reshape(N_HEADS * HEAD_DIM, n)


def _swa_constants(slopes):
    w = SWA_WINDOW
    s_rel = jnp.arange(2 * w, dtype=F32) - w
    e = jnp.zeros((2 * w, LANES), F32)
    e = e.at[:, 0:3].set(s_rel[:, None]).at[:, 3:6].set(1.0)
    t_rel = jnp.arange(w, dtype=F32)
    aug = jnp.zeros((N_HEADS, w, LANES), F32)
    for c, part in enumerate(_split3(slopes)):
        aug = aug.at[:, :, c].set(jnp.broadcast_to(part[:, None], (N_HEADS, w)))
    for c, part in enumerate(_split3(-slopes[:, None] * t_rel[None, :])):
        aug = aug.at[:, :, 3 + c].set(part)
    return e.astype(BF16), aug.astype(BF16)


def _moba_constants(slopes, seq):
    nblk = seq // MOBA_BLOCK
    pos = jnp.arange(seq)
    blk = pos // MOBA_BLOCK
    off = (pos % MOBA_BLOCK).astype(F32)
    base = (blk * MOBA_BLOCK).astype(F32)
    ncol = 2 * SUBLANES
    e = jnp.zeros((seq, LANES), F32)
    e = e.at[:, :ncol].set((blk[:, None] == jnp.arange(ncol)[None, :]).astype(F32))
    e = e.at[:, ncol:ncol + 3].set(off[:, None]).at[:, ncol + 3:ncol + 6].set(base[:, None])
    aug = jnp.zeros((N_HEADS, LANES), F32)
    for c, part in enumerate(_split3(slopes)):
        aug = aug.at[:, ncol + c].set(part).at[:, ncol + 3 + c].set(part)
    assert nblk <= ncol
    return e.astype(BF16), aug


def _ssm_weights(lam_re, lam_im, log_step, b_re, b_im, c_re, c_im):
    lr, li = lam_re.astype(F32), lam_im.astype(F32)
    dt = jnp.exp(log_step.astype(F32))[:, None]
    mag = jnp.exp(lr * dt)
    ab_re, ab_im = mag * jnp.cos(li * dt), mag * jnp.sin(li * dt)
    den = lr * lr + li * li
    nr, ni = ab_re - 1.0, ab_im
    f_re, f_im = (nr * lr + ni * li) / den, (ni * lr - nr * li) / den
    br, bi = b_re.astype(F32), b_im.astype(F32)
    bb_re = f_re[..., None] * br - f_im[..., None] * bi
    bb_im = f_re[..., None] * bi + f_im[..., None] * br
    half = SSM_GROUPS // 2
    eye = jnp.eye(half, dtype=F32)

    def in_map(bb):
        return jnp.stack([jnp.einsum('gph,gk->ghkp', bb[s * half:(s + 1) * half], eye)
                          .reshape(half * SSM_GROUP, half * SSM_STATE) for s in range(2)])

    def out_map(c):
        return jnp.stack([jnp.einsum('ghp,gk->kpgh', c[s * half:(s + 1) * half], eye)
                          .reshape(half * SSM_STATE, half * SSM_GROUP) for s in range(2)])

    a = jnp.stack([ab_re.reshape(-1), ab_im.reshape(-1)])
    return (in_map(bb_re).astype(BF16), in_map(bb_im).astype(BF16), a,
            out_map(c_re.astype(F32)).astype(BF16), out_map(-c_im.astype(F32)).astype(BF16))


def kernel(x, norm_mix, w_in, sinks, lam_re, lam_im, log_step, b_re, b_im, c_re, c_im, d_skip, w_glu,
           w_o_swa, w_o_ssm, w_o_moba, w_out, norm_ffn, w_ff1, w_ff2, norm_final):
    batch, seq, d = x.shape
    depth = w_in.shape[0]
    assert batch * 2 == SUBLANES, "the S5 scan packs two time steps of all batches into one vreg"
    qw = N_HEADS * HEAD_DIM
    kw = qw // KV_GROUP
    slopes = _alibi_slopes()
    swa_e, swa_aug = _swa_constants(slopes[:N_HEADS])
    moba_e, moba_aug = _moba_constants(slopes[N_HEADS:], seq)

    xs = x.transpose(1, 0, 2).reshape(seq * batch, d)
    for l in range(depth):
        wl = w_in[l]
        o = np.cumsum((0, qw, kw, kw, SSM_WIDTH, qw, kw, kw))
        w_proj = jnp.concatenate([
            _pair_cols(wl[:, o[0]:o[1]]) * ATTN_SCALE, wl[:, o[1]:o[3]], wl[:, o[3]:o[4]],
            _pair_cols(wl[:, o[4]:o[5]]) * ATTN_SCALE, wl[:, o[5]:o[7]]], axis=1).astype(BF16)
        w_gate = wl[:, o[7]:].astype(BF16)
        gain_mix = norm_mix[l].reshape(1, d)

        qa, ka, va, u, qm, km, vm = _norm_proj(xs, gain_mix, w_proj)

        sink_tab = jnp.broadcast_to(sinks[l].astype(F32)[:, None], (N_HEADS, LANES))
        oa = _swa(qa.reshape(seq, batch * qw), ka.reshape(seq, batch * kw), va.reshape(seq, batch * kw),
                  swa_e, swa_aug, sink_tab, batch).reshape(seq * batch, qw)

        wbr, wbi, a, cr, ci = _ssm_weights(lam_re[l], lam_im[l], log_step[l], b_re[l], b_im[l], c_re[l], c_im[l])
        gl = _ssm(u, wbr, wbi, a, cr, ci, d_skip[l].astype(F32).reshape(1, SSM_WIDTH),
                  w_glu[l].astype(BF16), batch)

        om = _moba(qm.reshape(seq, batch * qw), km.reshape(seq, batch * kw), vm.reshape(seq, batch * kw),
                   moba_e, moba_aug, batch).reshape(seq * batch, qw)

        xs = _merge(xs, oa, gl, om, gain_mix, w_gate, _pair_rows(w_o_swa[l]).astype(BF16),
                    w_o_ssm[l].astype(BF16), _pair_rows(w_o_moba[l]).astype(BF16), w_out[l].astype(BF16))
        xs = _ffn(xs, norm_ffn[l].reshape(1, d), w_ff1[l].astype(BF16), w_ff2[l].astype(BF16),
                  norm_final.reshape(1, d), final_norm=(l == depth - 1))
    return xs.reshape(seq, batch, d).transpose(1, 0, 2)
```

```python
import functools

import numpy as np
import jax
import jax.numpy as jnp
from jax import lax
from jax.experimental import pallas as pl
from jax.experimental.pallas import tpu as pltpu

F32 = jnp.float32
BF16 = jnp.bfloat16

HEAD_DIM = 64
N_HEADS = 8
KV_GROUP = 4
N_PAIRS = N_HEADS // 2
PAIR_ORDER = (0, 4, 1, 5, 2, 6, 3, 7)
ATTN_SCALE = HEAD_DIM ** -0.5
LOG2E = 1.4426950408889634
SWA_WINDOW = 128
MOBA_BLOCK = 256
MOBA_TOPK = 3
SSM_GROUPS = 32
SSM_GROUP = 16
SSM_STATE = 64
SSM_WIDTH = SSM_GROUPS * SSM_GROUP
SSM_LANES = SSM_GROUPS * SSM_STATE
NORM_EPS = 1e-6
ALIBI_MAX_BIAS = 8.0
MASK_VALUE = -1e30
M_INIT = 0.5 * MASK_VALUE
GATE_FLOOR = -3.0e38

LANES = 128
SUBLANES = 8
BF16_ROWS = 16
VMEM_LIMIT = 56 * 1024 * 1024

ROW_TILE = 512
SSM_CHUNK = 128
SWA_TILE = 512
SCAN_LANES = 512
MOBA_COLS = 2 * SUBLANES
MOBA_HI = HEAD_DIM
VT_ROWS = HEAD_DIM + BF16_ROWS

_NT = (((1,), (1,)), ((), ()))


def _params(*sem, flags=None):
    return pltpu.CompilerParams(dimension_semantics=sem, vmem_limit_bytes=VMEM_LIMIT, flags=flags)


def _resident(arr):
    zeros = (0,) * arr.ndim
    return pl.BlockSpec(arr.shape, lambda *_: zeros)


def _layer(arr, l):
    zeros = (0,) * (arr.ndim - 1)
    return pl.BlockSpec((None,) + arr.shape[1:], lambda *_: (l,) + zeros)


def _rms(x, gain):
    return x * lax.rsqrt(jnp.mean(x * x, axis=-1, keepdims=True) + NORM_EPS) * gain


def _split3(x):
    hi = x.astype(BF16).astype(F32)
    mid = (x - hi).astype(BF16).astype(F32)
    lo = (x - hi - mid).astype(BF16).astype(F32)
    return hi, mid, lo


def _low_lanes():
    return lax.broadcasted_iota(jnp.int32, (1, LANES), 1) < HEAD_DIM


_PROJ_SIZES = (512, 128, 128, 512, 512, 128, 128)


def _norm_proj_kernel(x_ref, g_ref, w_ref, qa_ref, ka_ref, va_ref, u_ref, qm_ref, km_ref, vm_ref):
    h = _rms(x_ref[...], g_ref[...]).astype(BF16)
    proj = jnp.dot(h, w_ref[...], preferred_element_type=F32)
    start = 0
    for ref, size in zip((qa_ref, ka_ref, va_ref, u_ref, qm_ref, km_ref, vm_ref), _PROJ_SIZES):
        ref[...] = proj[:, start:start + size].astype(ref.dtype)
        start += size


def _norm_proj(x, gain, w, l):
    rows, d = x.shape
    dts = (BF16, BF16, BF16, F32, BF16, BF16, BF16)
    return pl.pallas_call(
        _norm_proj_kernel,
        grid=(rows // ROW_TILE,),
        in_specs=[pl.BlockSpec((ROW_TILE, d), lambda i: (i, 0)), _layer(gain, l), _layer(w, l)],
        out_specs=[pl.BlockSpec((ROW_TILE, n), lambda i: (i, 0)) for n in _PROJ_SIZES],
        out_shape=[jax.ShapeDtypeStruct((rows, n), dt) for n, dt in zip(_PROJ_SIZES, dts)],
        compiler_params=_params("parallel"),
        name="norm_proj",
    )(x, gain, w)


def _swa_kernel(q_ref, k_ref, v_ref, e_ref, aug_ref, sink_ref, o_ref, kv_ref):
    qi = pl.program_id(1)
    w = SWA_WINDOW
    rows = N_PAIRS * w
    low = _low_lanes()

    @pl.when(qi == 0)
    def _():
        k = k_ref[...]
        v = v_ref[...]
        zero = jnp.zeros_like(k)
        kv_ref[0] = jnp.where(low, k, zero)
        kv_ref[1] = jnp.where(low, zero, k)
        kv_ref[2] = jnp.where(low, v, zero)
        kv_ref[3] = jnp.where(low, zero, v)

    row = lax.broadcasted_iota(jnp.int32, (rows, 2 * w), 0) & (w - 1)
    col = lax.broadcasted_iota(jnp.int32, (rows, 2 * w), 1)
    band = (col > row) & (col <= row + w)
    e_cols = e_ref[...]
    for blk in range(SWA_TILE // w):
        t0 = qi * SWA_TILE + blk * w
        prev = pl.multiple_of(jnp.maximum(t0 - w, 0), w)
        cur = pl.multiple_of(t0, w)
        mask = band & ((col >= w) | (t0 > 0))
        qs = jnp.concatenate([q_ref[blk * w:(blk + 1) * w, p * LANES:(p + 1) * LANES]
                              for p in range(N_PAIRS)], axis=0)
        out = None
        for half in range(2):
            kwin = jnp.concatenate([kv_ref[half, pl.ds(prev, w), :], kv_ref[half, pl.ds(cur, w), :]], axis=0)
            vwin = jnp.concatenate([kv_ref[2 + half, pl.ds(prev, w), :], kv_ref[2 + half, pl.ds(cur, w), :]],
                                   axis=0)
            qcat = jnp.concatenate([qs, aug_ref[half]], axis=1)
            kcat = jnp.concatenate([kwin, e_cols], axis=1)
            s = lax.dot_general(qcat, kcat, _NT, preferred_element_type=F32)
            s = jnp.where(mask, s, MASK_VALUE)
            sink = sink_ref[half]
            m = jnp.maximum(jnp.max(s, axis=1, keepdims=True), sink)
            p = jnp.exp(s - m)
            den = jnp.sum(p, axis=1, keepdims=True) + jnp.exp(sink - m)
            o = jnp.dot(p.astype(BF16), vwin, preferred_element_type=F32) * (1.0 / den)
            out = o if out is None else out + o
        for p in range(N_PAIRS):
            o_ref[blk * w:(blk + 1) * w, p * LANES:(p + 1) * LANES] = out[p * w:(p + 1) * w].astype(BF16)


def _swa(q, k, v, e_cols, aug, sinks, batch, l):
    seq = q.shape[0] // batch
    qw = N_HEADS * HEAD_DIM
    kw = k.shape[1]
    nq = seq // SWA_TILE
    return pl.pallas_call(
        _swa_kernel,
        grid=(batch, nq),
        in_specs=[pl.BlockSpec((SWA_TILE, qw), lambda b, i: (b * nq + i, 0)),
                  pl.BlockSpec((seq, kw), lambda b, i: (b, 0)),
                  pl.BlockSpec((seq, kw), lambda b, i: (b, 0)),
                  _resident(e_cols), _resident(aug), _layer(sinks, l)],
        out_specs=pl.BlockSpec((SWA_TILE, qw), lambda b, i: (b * nq + i, 0)),
        out_shape=jax.ShapeDtypeStruct(q.shape, BF16),
        scratch_shapes=[pltpu.VMEM((4, seq, kw), BF16)],
        compiler_params=_params("parallel", "arbitrary"),
        name="swa_attention",
    )(q, k, v, e_cols, aug, sinks)


def _select_bias(gate, blk, j):
    g = jnp.where(blk < j, gate, GATE_FLOOR)
    bias = jnp.where(blk == j, 0.0, MASK_VALUE)
    blk_f = blk.astype(F32)
    for _ in range(MOBA_TOPK):
        mx = jnp.max(g, axis=0, keepdims=True)
        idx = jnp.min(jnp.where(g == mx, blk_f, float(MOBA_COLS)), axis=0, keepdims=True)
        hit = (blk_f == idx) & (mx > GATE_FLOOR)
        bias = jnp.where(hit, 0.0, bias)
        g = jnp.where(hit, GATE_FLOOR, g)
    return bias


def _moba_kernel(q_ref, k_ref, v_ref, e_ref, aug_ref, o_ref, kc_ref, vt_ref, km_ref, qt_ref, cb_ref, *, nblk):
    j = pl.program_id(1)
    L = MOBA_BLOCK
    cols = N_PAIRS * L
    low = _low_lanes()

    @pl.when(j == 0)
    def _():
        k = k_ref[...]
        e = e_ref[...]
        kc_ref[0] = jnp.where(low, k, e)
        kc_ref[1] = jnp.where(low, e, k)
        vt = v_ref[...].astype(F32).T
        ones_row = jnp.where(lax.broadcasted_iota(jnp.int32, (BF16_ROWS, L), 0) == 0, 1.0, 0.0).astype(BF16)
        for n in range(nblk):
            for half in range(2):
                vt_ref[half, n, 0:HEAD_DIM, :] = vt[half * HEAD_DIM:(half + 1) * HEAD_DIM,
                                                    n * L:(n + 1) * L].astype(BF16)
                vt_ref[half, n, HEAD_DIM:, :] = ones_row
        kmean = jnp.mean(k.astype(F32).reshape(nblk, L, LANES), axis=1)
        km_ref[...] = jnp.zeros(km_ref.shape, BF16)
        for c, part in enumerate(_split3(kmean)):
            km_ref[c * BF16_ROWS:c * BF16_ROWS + nblk, :] = jnp.where(low, part, 0.0).astype(BF16)
            km_ref[(3 + c) * BF16_ROWS:(3 + c) * BF16_ROWS + nblk, :] = jnp.where(low, 0.0, part).astype(BF16)
        key = lax.broadcasted_iota(jnp.int32, (L, L), 0)
        qry = lax.broadcasted_iota(jnp.int32, (L, L), 1)
        cb_ref[0] = jnp.zeros((L, L), F32)
        cb_ref[1] = jnp.where(key <= qry, 0.0, MASK_VALUE)

    blk = lax.broadcasted_iota(jnp.int32, (MOBA_COLS, L), 0)
    for pair in range(N_PAIRS):
        qp = q_ref[:, pair * LANES:(pair + 1) * LANES]
        qpt = qp.astype(F32).T
        g = jnp.dot(km_ref[...], qpt.astype(BF16), preferred_element_type=F32)
        for half in range(2):
            r0 = 3 * BF16_ROWS * half
            gate = (g[r0:r0 + BF16_ROWS] + g[r0 + BF16_ROWS:r0 + 2 * BF16_ROWS]
                    + g[r0 + 2 * BF16_ROWS:r0 + 3 * BF16_ROWS])
            bias = _select_bias(gate, blk, j)
            slope = aug_ref[half * N_PAIRS + pair]
            extra = jnp.concatenate([bias, jnp.concatenate([slope] * (L // LANES), axis=1),
                                     jnp.zeros((HEAD_DIM - 2 * MOBA_COLS, L), F32)], axis=0)
            qh = qpt[half * HEAD_DIM:(half + 1) * HEAD_DIM]
            rows = [qh, extra] if half == 0 else [extra, qh]
            qt_ref[half, :, pair * L:(pair + 1) * L] = jnp.concatenate(rows, axis=0).astype(BF16)

    def visit(n, carry):
        cb = cb_ref[(n == j).astype(jnp.int32)]
        cb = jnp.concatenate([cb] * N_PAIRS, axis=1)
        off = pl.multiple_of(n * L, L)
        out = []
        for half in range(2):
            m, acc = carry[half]
            s = jnp.dot(kc_ref[half, pl.ds(off, L), :], qt_ref[half], preferred_element_type=F32) + cb
            m_new = jnp.maximum(m, jnp.max(s, axis=0, keepdims=True))
            p = jnp.exp2(s - m_new)
            pv = jnp.dot(vt_ref[half, n], p.astype(BF16), preferred_element_type=F32)
            out.append((m_new, jnp.exp2(m - m_new) * acc + pv))
        return tuple(out)

    init = tuple((jnp.full((1, cols), M_INIT, F32), jnp.zeros((VT_ROWS, cols), F32)) for _ in range(2))
    (_, acc_lo), (_, acc_hi) = lax.fori_loop(0, j + 1, visit, init)
    for pair in range(N_PAIRS):
        c = slice(pair * L, (pair + 1) * L)
        ot = jnp.concatenate([acc_lo[0:HEAD_DIM, c] / acc_lo[HEAD_DIM:HEAD_DIM + 1, c],
                              acc_hi[0:HEAD_DIM, c] / acc_hi[HEAD_DIM:HEAD_DIM + 1, c]], axis=0)
        o_ref[:, pair * LANES:(pair + 1) * LANES] = ot.T.astype(BF16)


def _moba(q, k, v, e_cols, aug, batch):
    seq = q.shape[0] // batch
    nblk = seq // MOBA_BLOCK
    qw = N_HEADS * HEAD_DIM
    kw = k.shape[1]
    return pl.pallas_call(
        functools.partial(_moba_kernel, nblk=nblk),
        grid=(batch, nblk),
        in_specs=[pl.BlockSpec((MOBA_BLOCK, qw), lambda b, i: (b * nblk + i, 0)),
                  pl.BlockSpec((seq, kw), lambda b, i: (b, 0)),
                  pl.BlockSpec((seq, kw), lambda b, i: (b, 0)),
                  _resident(e_cols), _resident(aug)],
        out_specs=pl.BlockSpec((MOBA_BLOCK, qw), lambda b, i: (b * nblk + i, 0)),
        out_shape=jax.ShapeDtypeStruct(q.shape, BF16),
        scratch_shapes=[pltpu.VMEM((2, seq, kw), BF16),
                        pltpu.VMEM((2, nblk, VT_ROWS, MOBA_BLOCK), BF16),
                        pltpu.VMEM((6 * BF16_ROWS, LANES), BF16),
                        pltpu.VMEM((2, LANES, N_PAIRS * MOBA_BLOCK), BF16),
                        pltpu.VMEM((2, MOBA_BLOCK, MOBA_BLOCK), F32)],
        compiler_params=_params("parallel", "arbitrary"),
        name="moba_attention",
    )(q, k, v, e_cols, aug)


def _ssm_kernel(u_ref, wbr_ref, wbi_ref, a_ref, cr_ref, ci_ref, d_ref, wglu_ref, o_ref,
                tb_ref, xr_ref, xi_ref, hr_ref, hi_ref):
    batch, steps, _ = u_ref.shape
    rows = batch * steps
    half_lanes = SSM_LANES // 2
    half_width = SSM_WIDTH // 2

    @pl.when(pl.program_id(0) == 0)
    def _():
        hr_ref[...] = jnp.zeros(hr_ref.shape, F32)
        hi_ref[...] = jnp.zeros(hi_ref.shape, F32)

    tiles = SSM_WIDTH // LANES
    for b in range(batch):
        ub_rows = u_ref[b]
        for c in range(tiles):
            tb_ref[c, pl.ds(b, steps, stride=batch), :] = ub_rows[:, c * LANES:(c + 1) * LANES]
    u = jnp.concatenate([tb_ref[c] for c in range(tiles)], axis=1)
    ub = u.astype(BF16)
    for s in range(2):
        us = ub[:, s * half_width:(s + 1) * half_width]
        xr_ref[:, s * half_lanes:(s + 1) * half_lanes] = jnp.dot(us, wbr_ref[s], preferred_element_type=F32)
        xi_ref[:, s * half_lanes:(s + 1) * half_lanes] = jnp.dot(us, wbi_ref[s], preferred_element_type=F32)

    first_step = lax.broadcasted_iota(jnp.int32, (SUBLANES, SCAN_LANES), 0) < batch
    for c in range(SSM_LANES // SCAN_LANES):
        lanes = slice(c * SCAN_LANES, (c + 1) * SCAN_LANES)
        ar = jnp.broadcast_to(a_ref[0:1, lanes], (SUBLANES, SCAN_LANES))
        ai = jnp.broadcast_to(a_ref[1:2, lanes], (SUBLANES, SCAN_LANES))

        def body(k, carry, lanes=lanes, ar=ar, ai=ai):
            pr, pi = carry
            rws = pl.ds(pl.multiple_of(k * SUBLANES, SUBLANES), SUBLANES)
            xr = xr_ref[rws, lanes]
            xi = xi_ref[rws, lanes]
            h1r = ar * pr - ai * pi + xr
            h1i = ar * pi + ai * pr + xi
            s1r = pltpu.roll(h1r, batch, axis=0)
            s1i = pltpu.roll(h1i, batch, axis=0)
            h2r = ar * s1r - ai * s1i + xr
            h2i = ar * s1i + ai * s1r + xi
            xr_ref[rws, lanes] = jnp.where(first_step, h1r, h2r)
            xi_ref[rws, lanes] = jnp.where(first_step, h1i, h2i)
            return pltpu.roll(h2r, batch, axis=0), pltpu.roll(h2i, batch, axis=0)

        pr, pi = lax.fori_loop(0, rows // SUBLANES, body, (hr_ref[:, lanes], hi_ref[:, lanes]))
        hr_ref[:, lanes] = pr
        hi_ref[:, lanes] = pi

    ys = []
    for s in range(2):
        hr = xr_ref[:, s * half_lanes:(s + 1) * half_lanes].astype(BF16)
        hi = xi_ref[:, s * half_lanes:(s + 1) * half_lanes].astype(BF16)
        ys.append(jnp.dot(hr, cr_ref[s], preferred_element_type=F32)
                  + jnp.dot(hi, ci_ref[s], preferred_element_type=F32))
    y = jnp.concatenate(ys, axis=1) + d_ref[...] * u
    z = jnp.dot(jax.nn.gelu(y).astype(BF16), wglu_ref[...], preferred_element_type=F32)
    out = z[:, :SSM_WIDTH] * jax.nn.sigmoid(z[:, SSM_WIDTH:])
    for c in range(tiles):
        tb_ref[c] = out[:, c * LANES:(c + 1) * LANES]
    for b in range(batch):
        o_ref[b] = jnp.concatenate([tb_ref[c, pl.ds(b, steps, stride=batch), :] for c in range(tiles)],
                                   axis=1).astype(BF16)


def _ssm(u, wbr, wbi, a, cr, ci, d, wglu, batch, l):
    seq = u.shape[0] // batch
    rows = SSM_CHUNK * batch
    block = pl.BlockSpec((batch, SSM_CHUNK, SSM_WIDTH), lambda i: (0, i, 0))
    out = pl.pallas_call(
        _ssm_kernel,
        grid=(seq // SSM_CHUNK,),
        in_specs=[block] + [_layer(t, l) for t in (wbr, wbi, a, cr, ci, d, wglu)],
        out_specs=block,
        out_shape=jax.ShapeDtypeStruct((batch, seq, SSM_WIDTH), BF16),
        scratch_shapes=[pltpu.VMEM((SSM_WIDTH // LANES, rows, LANES), F32),
                        pltpu.VMEM((rows, SSM_LANES), F32), pltpu.VMEM((rows, SSM_LANES), F32),
                        pltpu.VMEM((SUBLANES, SSM_LANES), F32), pltpu.VMEM((SUBLANES, SSM_LANES), F32)],
        compiler_params=_params("arbitrary"),
        name="s5_ssm",
    )(u.reshape(batch, seq, SSM_WIDTH), wbr, wbi, a, cr, ci, d, wglu)
    return out.reshape(batch * seq, SSM_WIDTH)


def _merge_kernel(x_ref, oa_ref, gl_ref, om_ref, g_ref, wg_ref, woa_ref, wos_ref, wom_ref, wout_ref, o_ref):
    x = x_ref[...]
    d = x.shape[1]
    h = _rms(x, g_ref[...]).astype(BF16)
    mixed = None
    for n, (b_ref, w_ref) in enumerate(((oa_ref, woa_ref), (gl_ref, wos_ref), (om_ref, wom_ref))):
        gate = jax.nn.sigmoid(jnp.dot(h, wg_ref[:, n * d:(n + 1) * d], preferred_element_type=F32))
        term = gate * jnp.dot(b_ref[...], w_ref[...], preferred_element_type=F32)
        mixed = term if mixed is None else mixed + term
    o_ref[...] = x + jnp.dot(mixed.astype(BF16), wout_ref[...], preferred_element_type=F32)


def _merge(x, oa, gl, om, gain, wg, woa, wos, wom, wout, l):
    rows, d = x.shape
    tile = lambda n: pl.BlockSpec((ROW_TILE, n), lambda i: (i, 0))
    return pl.pallas_call(
        _merge_kernel,
        grid=(rows // ROW_TILE,),
        in_specs=[tile(d), tile(oa.shape[1]), tile(gl.shape[1]), tile(om.shape[1])]
                 + [_layer(t, l) for t in (gain, wg, woa, wos, wom, wout)],
        out_specs=tile(d),
        out_shape=jax.ShapeDtypeStruct((rows, d), F32),
        compiler_params=_params("parallel"),
        name="merge",
    )(x, oa, gl, om, gain, wg, woa, wos, wom, wout)


def _ffn_kernel(x_ref, g_ref, w1_ref, w2_ref, gf_ref, o_ref, *, final_norm):
    x = x_ref[...]
    d = x.shape[1]
    h = _rms(x, g_ref[...]).astype(BF16)
    acc = x
    for c in range(w1_ref.shape[1] // d):
        a = jnp.maximum(jnp.dot(h, w1_ref[:, c * d:(c + 1) * d], preferred_element_type=F32), 0.0)
        acc = acc + jnp.dot((a * a).astype(BF16), w2_ref[c * d:(c + 1) * d, :], preferred_element_type=F32)
    o_ref[...] = _rms(acc, gf_ref[...]) if final_norm else acc


def _ffn(x, gain, w1, w2, gain_final, l, final_norm):
    rows, d = x.shape
    return pl.pallas_call(
        functools.partial(_ffn_kernel, final_norm=final_norm),
        grid=(rows // ROW_TILE,),
        in_specs=[pl.BlockSpec((ROW_TILE, d), lambda i: (i, 0)),
                  _layer(gain, l), _layer(w1, l), _layer(w2, l), _resident(gain_final)],
        out_specs=pl.BlockSpec((ROW_TILE, d), lambda i: (i, 0)),
        out_shape=jax.ShapeDtypeStruct((rows, d), F32),
        compiler_params=_params("parallel"),
        name="ffn",
    )(x, gain, w1, w2, gain_final)


def _alibi_slopes():
    n = 2 * N_HEADS
    return jnp.asarray(2.0 ** (-ALIBI_MAX_BIAS * np.arange(1, n + 1) / n), F32)


def _pair_cols(w):
    return w.reshape(w.shape[:-1] + (N_HEADS, HEAD_DIM))[..., PAIR_ORDER, :].reshape(w.shape)


def _pair_rows(w):
    return w.reshape(w.shape[:-2] + (N_HEADS, HEAD_DIM, w.shape[-1]))[..., PAIR_ORDER, :, :].reshape(w.shape)


def _swa_constants(slopes):
    w = SWA_WINDOW
    s_rel = jnp.arange(2 * w, dtype=F32) - w
    e = jnp.zeros((2 * w, LANES), F32)
    e = e.at[:, 0:3].set(s_rel[:, None]).at[:, 3:6].set(1.0)
    t_rel = jnp.arange(w, dtype=F32)
    aug = jnp.zeros((N_HEADS, w, LANES), F32)
    for c, part in enumerate(_split3(slopes)):
        aug = aug.at[:, :, c].set(jnp.broadcast_to(part[:, None], (N_HEADS, w)))
    for c, part in enumerate(_split3(-slopes[:, None] * t_rel[None, :])):
        aug = aug.at[:, :, 3 + c].set(part)
    return e.astype(BF16), aug.reshape(2, N_PAIRS * w, LANES).astype(BF16)


def _moba_constants(slopes, seq):
    assert seq // MOBA_BLOCK <= MOBA_COLS
    pos = jnp.arange(seq)
    blk = pos // MOBA_BLOCK
    off = (pos % MOBA_BLOCK).astype(F32)
    base = (blk * MOBA_BLOCK).astype(F32)
    n = MOBA_COLS
    onehot = (blk[:, None] == jnp.arange(n)[None, :]).astype(F32)
    e = jnp.zeros((seq, LANES), F32)
    for o in (MOBA_HI, 0):
        e = e.at[:, o:o + n].set(onehot)
        e = e.at[:, o + n:o + n + 3].set(off[:, None]).at[:, o + n + 3:o + n + 6].set(base[:, None])
    aug = jnp.zeros((N_HEADS, MOBA_COLS, LANES), F32)
    for c, part in enumerate(_split3(slopes * LOG2E)):
        aug = aug.at[:, c, :].set(part[:, None]).at[:, 3 + c, :].set(part[:, None])
    return e.astype(BF16), aug


def _ssm_weights(lam_re, lam_im, log_step, b_re, b_im, c_re, c_im):
    lr, li = lam_re.astype(F32), lam_im.astype(F32)
    dt = jnp.exp(log_step.astype(F32))[:, None]
    mag = jnp.exp(lr * dt)
    ab_re, ab_im = mag * jnp.cos(li * dt), mag * jnp.sin(li * dt)
    den = lr * lr + li * li
    nr, ni = ab_re - 1.0, ab_im
    f_re, f_im = (nr * lr + ni * li) / den, (ni * lr - nr * li) / den
    br, bi = b_re.astype(F32), b_im.astype(F32)
    bb_re = f_re[..., None] * br - f_im[..., None] * bi
    bb_im = f_re[..., None] * bi + f_im[..., None] * br
    half = SSM_GROUPS // 2
    eye = jnp.eye(half, dtype=F32)

    def in_map(bb):
        return jnp.stack([jnp.einsum('gph,gk->ghkp', bb[s * half:(s + 1) * half], eye)
                          .reshape(half * SSM_GROUP, half * SSM_STATE) for s in range(2)])

    def out_map(c):
        return jnp.stack([jnp.einsum('ghp,gk->kpgh', c[s * half:(s + 1) * half], eye)
                          .reshape(half * SSM_STATE, half * SSM_GROUP) for s in range(2)])

    a = jnp.stack([ab_re.reshape(-1), ab_im.reshape(-1)])
    return (in_map(bb_re).astype(BF16), in_map(bb_im).astype(BF16), a,
            out_map(c_re.astype(F32)).astype(BF16), out_map(-c_im.astype(F32)).astype(BF16))


def kernel(x, norm_mix, w_in, sinks, lam_re, lam_im, log_step, b_re, b_im, c_re, c_im, d_skip, w_glu,
           w_o_swa, w_o_ssm, w_o_moba, w_out, norm_ffn, w_ff1, w_ff2, norm_final):
    batch, seq, d = x.shape
    depth = w_in.shape[0]
    assert batch * 2 == SUBLANES, "the S5 scan packs two time steps of all batches into one vreg"
    qw = N_HEADS * HEAD_DIM
    kw = qw // KV_GROUP
    slopes = _alibi_slopes()
    swa_e, swa_aug = _swa_constants(slopes[:N_HEADS])
    moba_e, moba_aug = _moba_constants(slopes[N_HEADS:], seq)

    o = np.cumsum((0, qw, kw, kw, SSM_WIDTH, qw, kw, kw))
    w_proj = jnp.concatenate([
        _pair_cols(w_in[..., o[0]:o[1]]) * ATTN_SCALE, w_in[..., o[1]:o[4]],
        _pair_cols(w_in[..., o[4]:o[5]]) * (ATTN_SCALE * LOG2E), w_in[..., o[5]:o[7]]], axis=-1).astype(BF16)
    w_gate = w_in[..., o[7]:].astype(BF16)
    gain_mix = norm_mix.reshape(depth, 1, d)
    gain_ffn = norm_ffn.reshape(depth, 1, d)
    gain_final = norm_final.reshape(1, d)
    sink_tab = jnp.repeat(sinks.astype(F32), SWA_WINDOW, axis=1).reshape(depth, 2, N_PAIRS * SWA_WINDOW, 1)
    wbr, wbi, a, cr, ci = jax.vmap(_ssm_weights)(lam_re, lam_im, log_step, b_re, b_im, c_re, c_im)
    skip = d_skip.astype(F32).reshape(depth, 1, SSM_WIDTH)
    wglu = w_glu.astype(BF16)
    woa, wos, wom = _pair_rows(w_o_swa).astype(BF16), w_o_ssm.astype(BF16), _pair_rows(w_o_moba).astype(BF16)
    wout, w1, w2 = w_out.astype(BF16), w_ff1.astype(BF16), w_ff2.astype(BF16)

    xs = x.reshape(batch * seq, d)
    for l in range(depth):
        qa, ka, va, u, qm, km, vm = _norm_proj(xs, gain_mix, w_proj, l)
        oa = _swa(qa, ka, va, swa_e, swa_aug, sink_tab, batch, l)
        gl = _ssm(u, wbr, wbi, a, cr, ci, skip, wglu, batch, l)
        om = _moba(qm, km, vm, moba_e, moba_aug, batch)
        xs = _merge(xs, oa, gl, om, gain_mix, w_gate, woa, wos, wom, wout, l)
        xs = _ffn(xs, gain_ffn, w1, w2, gain_final, l, final_norm=(l == depth - 1))
    return xs.reshape(batch, seq, d)
```

```python
import functools

import numpy as np
import jax
import jax.numpy as jnp
from jax import lax
from jax.experimental import pallas as pl
from jax.experimental.pallas import tpu as pltpu

F32 = jnp.float32
BF16 = jnp.bfloat16

HEAD_DIM = 64
N_HEADS = 8
KV_GROUP = 4
N_PAIRS = N_HEADS // 2
PAIR_ORDER = (0, 4, 1, 5, 2, 6, 3, 7)
ATTN_SCALE = HEAD_DIM ** -0.5
LOG2E = 1.4426950408889634
SWA_WINDOW = 128
MOBA_BLOCK = 256
MOBA_TOPK = 3
SSM_GROUPS = 32
SSM_GROUP = 16
SSM_STATE = 64
SSM_WIDTH = SSM_GROUPS * SSM_GROUP
SSM_LANES = SSM_GROUPS * SSM_STATE
NORM_EPS = 1e-6
ALIBI_MAX_BIAS = 8.0
MASK_VALUE = -1e30
M_INIT = 0.5 * MASK_VALUE
GATE_FLOOR = -3.0e38

LANES = 128
SUBLANES = 8
BF16_ROWS = 16
VMEM_LIMIT = 56 * 1024 * 1024

ROW_TILE = 512
SSM_CHUNK = 128
SWA_TILE = 512
SCAN_LANES = 512
MOBA_COLS = 2 * SUBLANES
MOBA_HI = HEAD_DIM
VT_ROWS = HEAD_DIM + BF16_ROWS
VISIT = 2

_NT = (((1,), (1,)), ((), ()))


def _params(*sem, flags=None):
    return pltpu.CompilerParams(dimension_semantics=sem, vmem_limit_bytes=VMEM_LIMIT, flags=flags)


def _resident(arr):
    zeros = (0,) * arr.ndim
    return pl.BlockSpec(arr.shape, lambda *_: zeros)


def _layer(arr, l):
    zeros = (0,) * (arr.ndim - 1)
    return pl.BlockSpec((None,) + arr.shape[1:], lambda *_: (l,) + zeros)


def _rms(x, gain):
    return x * lax.rsqrt(jnp.mean(x * x, axis=-1, keepdims=True) + NORM_EPS) * gain


def _split3(x):
    hi = x.astype(BF16).astype(F32)
    mid = (x - hi).astype(BF16).astype(F32)
    lo = (x - hi - mid).astype(BF16).astype(F32)
    return hi, mid, lo


def _low_lanes():
    return lax.broadcasted_iota(jnp.int32, (1, LANES), 1) < HEAD_DIM


_PROJ_SIZES = (512, 128, 128, 512, 512, 128, 128)


def _norm_proj_kernel(x_ref, g_ref, w_ref, qa_ref, ka_ref, va_ref, u_ref, qm_ref, km_ref, vm_ref):
    h = _rms(x_ref[...], g_ref[...]).astype(BF16)
    proj = jnp.dot(h, w_ref[...], preferred_element_type=F32)
    start = 0
    for ref, size in zip((qa_ref, ka_ref, va_ref, u_ref, qm_ref, km_ref, vm_ref), _PROJ_SIZES):
        ref[...] = proj[:, start:start + size].astype(ref.dtype)
        start += size


def _norm_proj(x, gain, w, l):
    rows, d = x.shape
    dts = (BF16, BF16, BF16, F32, BF16, BF16, BF16)
    return pl.pallas_call(
        _norm_proj_kernel,
        grid=(rows // ROW_TILE,),
        in_specs=[pl.BlockSpec((ROW_TILE, d), lambda i: (i, 0)), _layer(gain, l), _layer(w, l)],
        out_specs=[pl.BlockSpec((ROW_TILE, n), lambda i: (i, 0)) for n in _PROJ_SIZES],
        out_shape=[jax.ShapeDtypeStruct((rows, n), dt) for n, dt in zip(_PROJ_SIZES, dts)],
        compiler_params=_params("parallel"),
        name="norm_proj",
    )(x, gain, w)


def _swa_kernel(q_ref, k_ref, v_ref, e_ref, aug_ref, sink_ref, o_ref, kc_ref, vt_ref, qt_ref, mb_ref):
    qi = pl.program_id(1)
    w = SWA_WINDOW
    half_cols = N_PAIRS * w
    low = _low_lanes()

    @pl.when(qi == 0)
    def _():
        k = k_ref[...]
        zero = jnp.zeros_like(k)
        kc_ref[:, 0:LANES] = jnp.where(low, k, zero)
        kc_ref[:, LANES:] = jnp.where(low, zero, k)
        vt = v_ref[...].astype(F32).T
        ones_row = jnp.where(lax.broadcasted_iota(jnp.int32, (BF16_ROWS, w), 0) == 0, 1.0, 0.0).astype(BF16)
        for n in range(vt_ref.shape[0]):
            for half in range(2):
                r0 = half * VT_ROWS
                vt_ref[n, r0:r0 + HEAD_DIM, :] = vt[half * HEAD_DIM:(half + 1) * HEAD_DIM,
                                                    n * w:(n + 1) * w].astype(BF16)
                vt_ref[n, r0 + HEAD_DIM:r0 + VT_ROWS, :] = ones_row
        qt_ref[...] = jnp.zeros(qt_ref.shape, BF16)
        key = lax.broadcasted_iota(jnp.int32, (2 * w, w), 0)
        qry = lax.broadcasted_iota(jnp.int32, (2 * w, w), 1)
        band = (key > qry) & (key <= qry + w)
        mb_ref[0] = jnp.where(band, 0.0, MASK_VALUE)
        mb_ref[1] = jnp.where(band & (key >= w), 0.0, MASK_VALUE)

    lane2 = lax.broadcasted_iota(jnp.int32, (1, 2 * LANES), 1)
    e_lanes = (lane2 >= HEAD_DIM) & (lane2 < 2 * LANES - HEAD_DIM)
    e_cols = e_ref[...]
    sink = sink_ref[...]
    for blk in range(SWA_TILE // w):
        t0 = qi * SWA_TILE + blk * w
        prev = jnp.maximum(t0 - w, 0)
        kwin = jnp.concatenate([kc_ref[pl.ds(pl.multiple_of(prev, w), w), :],
                                kc_ref[pl.ds(pl.multiple_of(t0, w), w), :]], axis=0)
        kcat = jnp.where(e_lanes, e_cols, kwin)
        vtw = jnp.concatenate([vt_ref[prev // w], vt_ref[t0 // w]], axis=1)
        for pair in range(N_PAIRS):
            qpt = q_ref[blk * w:(blk + 1) * w, pair * LANES:(pair + 1) * LANES].astype(F32).T
            c = pair * w
            qt_ref[blk, 0:LANES, c:c + w] = jnp.concatenate([qpt[0:HEAD_DIM], aug_ref[0, pair]],
                                                           axis=0).astype(BF16)
            qt_ref[blk, LANES:, half_cols + c:half_cols + c + w] = jnp.concatenate(
                [aug_ref[1, pair], qpt[HEAD_DIM:]], axis=0).astype(BF16)
        mb = mb_ref[(t0 == 0).astype(jnp.int32)]
        s = jnp.dot(kcat, qt_ref[blk], preferred_element_type=F32) + jnp.concatenate([mb] * N_HEADS, axis=1)
        m = jnp.maximum(jnp.max(s, axis=0, keepdims=True), sink)
        p = jnp.exp2((s - m).astype(BF16))
        pv = jnp.concatenate(
            [jnp.dot(vtw[half * VT_ROWS:(half + 1) * VT_ROWS], p[:, half * half_cols:(half + 1) * half_cols],
                     preferred_element_type=F32) for half in range(2)], axis=1)
        o_t = pv[0:HEAD_DIM] / (pv[HEAD_DIM:HEAD_DIM + 1] + jnp.exp2(sink - m))
        for pair in range(N_PAIRS):
            c = pair * w
            both = jnp.concatenate([o_t[:, c:c + w], o_t[:, half_cols + c:half_cols + c + w]], axis=0)
            o_ref[blk * w:(blk + 1) * w, pair * LANES:(pair + 1) * LANES] = both.T.astype(BF16)


def _swa(q, k, v, e_cols, aug, sinks, batch, l):
    seq = q.shape[0] // batch
    qw = N_HEADS * HEAD_DIM
    kw = k.shape[1]
    nq = seq // SWA_TILE
    nwin = SWA_TILE // SWA_WINDOW
    return pl.pallas_call(
        _swa_kernel,
        grid=(batch, nq),
        in_specs=[pl.BlockSpec((SWA_TILE, qw), lambda b, i: (b * nq + i, 0)),
                  pl.BlockSpec((seq, kw), lambda b, i: (b, 0)),
                  pl.BlockSpec((seq, kw), lambda b, i: (b, 0)),
                  _resident(e_cols), _resident(aug), _layer(sinks, l)],
        out_specs=pl.BlockSpec((SWA_TILE, qw), lambda b, i: (b * nq + i, 0)),
        out_shape=jax.ShapeDtypeStruct(q.shape, BF16),
        scratch_shapes=[pltpu.VMEM((seq, 2 * kw), BF16),
                        pltpu.VMEM((seq // SWA_WINDOW, 2 * VT_ROWS, SWA_WINDOW), BF16),
                        pltpu.VMEM((nwin, 2 * LANES, N_HEADS * SWA_WINDOW), BF16),
                        pltpu.VMEM((2, 2 * SWA_WINDOW, SWA_WINDOW), F32)],
        compiler_params=_params("parallel", "arbitrary"),
        name="swa_attention",
    )(q, k, v, e_cols, aug, sinks)


def _select_bias(gate, blk, j):
    g = jnp.where(blk < j, gate, GATE_FLOOR)
    bias = jnp.where(blk == j, 0.0, MASK_VALUE)
    blk_f = blk.astype(F32)
    for _ in range(MOBA_TOPK):
        mx = jnp.max(g, axis=0, keepdims=True)
        idx = jnp.min(jnp.where(g == mx, blk_f, float(MOBA_COLS)), axis=0, keepdims=True)
        hit = (blk_f == idx) & (mx > GATE_FLOOR)
        bias = jnp.where(hit, 0.0, bias)
        g = jnp.where(hit, GATE_FLOOR, g)
    return bias


def _moba_kernel(q_ref, k_ref, v_ref, e_ref, aug_ref, o_ref, kc_ref, vt_ref, km_ref, qt_ref, cb_ref, *, nblk):
    j = pl.program_id(1)
    L = MOBA_BLOCK
    cols = N_PAIRS * L
    low = _low_lanes()

    @pl.when(j == 0)
    def _():
        k = k_ref[...]
        e = e_ref[...]
        kc_ref[:, 0:LANES] = jnp.where(low, k, e)
        kc_ref[:, LANES:] = jnp.where(low, e, k)
        vt = v_ref[...].astype(F32).T
        ones_row = jnp.where(lax.broadcasted_iota(jnp.int32, (BF16_ROWS, VISIT * L), 0) == 0,
                             1.0, 0.0).astype(BF16)
        for n in range(nblk // VISIT):
            for half in range(2):
                r0 = half * VT_ROWS
                vt_ref[n, r0:r0 + HEAD_DIM, :] = vt[half * HEAD_DIM:(half + 1) * HEAD_DIM,
                                                    n * VISIT * L:(n + 1) * VISIT * L].astype(BF16)
                vt_ref[n, r0 + HEAD_DIM:r0 + VT_ROWS, :] = ones_row
        qt_ref[...] = jnp.zeros(qt_ref.shape, BF16)
        kmean = jnp.mean(k.astype(F32).reshape(nblk, L, LANES), axis=1)
        km_ref[...] = jnp.zeros(km_ref.shape, BF16)
        for c, part in enumerate(_split3(kmean)):
            km_ref[c * BF16_ROWS:c * BF16_ROWS + nblk, :] = jnp.where(low, part, 0.0).astype(BF16)
            km_ref[(3 + c) * BF16_ROWS:(3 + c) * BF16_ROWS + nblk, :] = jnp.where(low, 0.0, part).astype(BF16)
        key = lax.broadcasted_iota(jnp.int32, (L, L), 0)
        qry = lax.broadcasted_iota(jnp.int32, (L, L), 1)
        cb_ref[0] = jnp.zeros((L, L), F32)
        cb_ref[1] = jnp.where(key <= qry, 0.0, MASK_VALUE)

    blk = lax.broadcasted_iota(jnp.int32, (MOBA_COLS, L), 0)
    for pair in range(N_PAIRS):
        qp = q_ref[:, pair * LANES:(pair + 1) * LANES]
        qpt = qp.astype(F32).T
        g = jnp.dot(km_ref[...], qpt.astype(BF16), preferred_element_type=F32)
        for half in range(2):
            r0 = 3 * BF16_ROWS * half
            gate = (g[r0:r0 + BF16_ROWS] + g[r0 + BF16_ROWS:r0 + 2 * BF16_ROWS]
                    + g[r0 + 2 * BF16_ROWS:r0 + 3 * BF16_ROWS])
            bias = _select_bias(gate, blk, j)
            slope = aug_ref[half * N_PAIRS + pair]
            extra = jnp.concatenate([bias, jnp.concatenate([slope] * (L // LANES), axis=1),
                                     jnp.zeros((HEAD_DIM - 2 * MOBA_COLS, L), F32)], axis=0)
            qh = qpt[half * HEAD_DIM:(half + 1) * HEAD_DIM]
            rows = [qh, extra] if half == 0 else [extra, qh]
            c0 = half * cols + pair * L
            qt_ref[half * LANES:(half + 1) * LANES, c0:c0 + L] = jnp.concatenate(rows, axis=0).astype(BF16)

    def visit(n, carry):
        m, acc = carry
        cb = jnp.concatenate([cb_ref[(VISIT * n + i == j).astype(jnp.int32)] for i in range(VISIT)], axis=0)
        cb = jnp.concatenate([cb] * N_HEADS, axis=1)
        off = pl.multiple_of(n * (VISIT * L), VISIT * L)
        s = jnp.dot(kc_ref[pl.ds(off, VISIT * L), :], qt_ref[...], preferred_element_type=F32) + cb
        m_new = jnp.maximum(m, jnp.max(s, axis=0, keepdims=True))
        p = jnp.exp2((s - m_new).astype(BF16))
        pv = jnp.concatenate(
            [jnp.dot(vt_ref[n, half * VT_ROWS:(half + 1) * VT_ROWS, :], p[:, half * cols:(half + 1) * cols],
                     preferred_element_type=F32) for half in range(2)], axis=1)
        return m_new, jnp.exp2(m - m_new) * acc + pv

    init = (jnp.full((1, 2 * cols), M_INIT, F32), jnp.zeros((VT_ROWS, 2 * cols), F32))
    _, acc = lax.fori_loop(0, j // VISIT + 1, visit, init)
    for pair in range(N_PAIRS):
        parts = []
        for half in range(2):
            a = acc[:, half * cols + pair * L:half * cols + (pair + 1) * L]
            parts.append(a[0:HEAD_DIM] / a[HEAD_DIM:HEAD_DIM + 1])
        o_ref[:, pair * LANES:(pair + 1) * LANES] = jnp.concatenate(parts, axis=0).T.astype(BF16)


def _moba(q, k, v, e_cols, aug, batch):
    seq = q.shape[0] // batch
    nblk = seq // MOBA_BLOCK
    qw = N_HEADS * HEAD_DIM
    kw = k.shape[1]
    return pl.pallas_call(
        functools.partial(_moba_kernel, nblk=nblk),
        grid=(batch, nblk),
        in_specs=[pl.BlockSpec((MOBA_BLOCK, qw), lambda b, i: (b * nblk + i, 0)),
                  pl.BlockSpec((seq, kw), lambda b, i: (b, 0)),
                  pl.BlockSpec((seq, kw), lambda b, i: (b, 0)),
                  _resident(e_cols), _resident(aug)],
        out_specs=pl.BlockSpec((MOBA_BLOCK, qw), lambda b, i: (b * nblk + i, 0)),
        out_shape=jax.ShapeDtypeStruct(q.shape, BF16),
        scratch_shapes=[pltpu.VMEM((seq, 2 * kw), BF16),
                        pltpu.VMEM((nblk // VISIT, 2 * VT_ROWS, VISIT * MOBA_BLOCK), BF16),
                        pltpu.VMEM((6 * BF16_ROWS, LANES), BF16),
                        pltpu.VMEM((2 * LANES, N_HEADS * MOBA_BLOCK), BF16),
                        pltpu.VMEM((2, MOBA_BLOCK, MOBA_BLOCK), F32)],
        compiler_params=_params("parallel", "arbitrary"),
        name="moba_attention",
    )(q, k, v, e_cols, aug)


def _ssm_kernel(u_ref, wbr_ref, wbi_ref, a_ref, cr_ref, ci_ref, d_ref, wglu_ref, o_ref,
                tb_ref, xr_ref, xi_ref, hr_ref, hi_ref):
    batch, steps, _ = u_ref.shape
    rows = batch * steps
    half_lanes = SSM_LANES // 2
    half_width = SSM_WIDTH // 2

    @pl.when(pl.program_id(0) == 0)
    def _():
        hr_ref[...] = jnp.zeros(hr_ref.shape, F32)
        hi_ref[...] = jnp.zeros(hi_ref.shape, F32)

    tiles = SSM_WIDTH // LANES
    for b in range(batch):
        ub_rows = u_ref[b]
        for c in range(tiles):
            tb_ref[c, pl.ds(b, steps, stride=batch), :] = ub_rows[:, c * LANES:(c + 1) * LANES]
    u = jnp.concatenate([tb_ref[c] for c in range(tiles)], axis=1)
    ub = u.astype(BF16)
    for s in range(2):
        us = ub[:, s * half_width:(s + 1) * half_width]
        xr_ref[:, s * half_lanes:(s + 1) * half_lanes] = jnp.dot(us, wbr_ref[s], preferred_element_type=F32)
        xi_ref[:, s * half_lanes:(s + 1) * half_lanes] = jnp.dot(us, wbi_ref[s], preferred_element_type=F32)

    first_step = lax.broadcasted_iota(jnp.int32, (SUBLANES, SCAN_LANES), 0) < batch
    for c in range(SSM_LANES // SCAN_LANES):
        lanes = slice(c * SCAN_LANES, (c + 1) * SCAN_LANES)
        ar = jnp.broadcast_to(a_ref[0:1, lanes], (SUBLANES, SCAN_LANES))
        ai = jnp.broadcast_to(a_ref[1:2, lanes], (SUBLANES, SCAN_LANES))

        def body(k, carry, lanes=lanes, ar=ar, ai=ai):
            pr, pi = carry
            rws = pl.ds(pl.multiple_of(k * SUBLANES, SUBLANES), SUBLANES)
            xr = xr_ref[rws, lanes]
            xi = xi_ref[rws, lanes]
            h1r = ar * pr - ai * pi + xr
            h1i = ar * pi + ai * pr + xi
            s1r = pltpu.roll(h1r, batch, axis=0)
            s1i = pltpu.roll(h1i, batch, axis=0)
            h2r = ar * s1r - ai * s1i + xr
            h2i = ar * s1i + ai * s1r + xi
            xr_ref[rws, lanes] = jnp.where(first_step, h1r, h2r)
            xi_ref[rws, lanes] = jnp.where(first_step, h1i, h2i)
            return pltpu.roll(h2r, batch, axis=0), pltpu.roll(h2i, batch, axis=0)

        pr, pi = lax.fori_loop(0, rows // SUBLANES, body, (hr_ref[:, lanes], hi_ref[:, lanes]))
        hr_ref[:, lanes] = pr
        hi_ref[:, lanes] = pi

    ys = []
    for s in range(2):
        hr = xr_ref[:, s * half_lanes:(s + 1) * half_lanes].astype(BF16)
        hi = xi_ref[:, s * half_lanes:(s + 1) * half_lanes].astype(BF16)
        ys.append(jnp.dot(hr, cr_ref[s], preferred_element_type=F32)
                  + jnp.dot(hi, ci_ref[s], preferred_element_type=F32))
    y = jnp.concatenate(ys, axis=1) + d_ref[...] * u
    z = jnp.dot(jax.nn.gelu(y).astype(BF16), wglu_ref[...], preferred_element_type=F32)
    out = z[:, :SSM_WIDTH] * jax.nn.sigmoid(z[:, SSM_WIDTH:])
    for c in range(tiles):
        tb_ref[c] = out[:, c * LANES:(c + 1) * LANES]
    for b in range(batch):
        o_ref[b] = jnp.concatenate([tb_ref[c, pl.ds(b, steps, stride=batch), :] for c in range(tiles)],
                                   axis=1).astype(BF16)


def _ssm(u, wbr, wbi, a, cr, ci, d, wglu, batch, l):
    seq = u.shape[0] // batch
    rows = SSM_CHUNK * batch
    block = pl.BlockSpec((batch, SSM_CHUNK, SSM_WIDTH), lambda i: (0, i, 0))
    out = pl.pallas_call(
        _ssm_kernel,
        grid=(seq // SSM_CHUNK,),
        in_specs=[block] + [_layer(t, l) for t in (wbr, wbi, a, cr, ci, d, wglu)],
        out_specs=block,
        out_shape=jax.ShapeDtypeStruct((batch, seq, SSM_WIDTH), BF16),
        scratch_shapes=[pltpu.VMEM((SSM_WIDTH // LANES, rows, LANES), F32),
                        pltpu.VMEM((rows, SSM_LANES), F32), pltpu.VMEM((rows, SSM_LANES), F32),
                        pltpu.VMEM((SUBLANES, SSM_LANES), F32), pltpu.VMEM((SUBLANES, SSM_LANES), F32)],
        compiler_params=_params("arbitrary"),
        name="s5_ssm",
    )(u.reshape(batch, seq, SSM_WIDTH), wbr, wbi, a, cr, ci, d, wglu)
    return out.reshape(batch * seq, SSM_WIDTH)


def _merge_kernel(x_ref, oa_ref, gl_ref, om_ref, g_ref, wg_ref, woa_ref, wos_ref, wom_ref, wout_ref, o_ref):
    x = x_ref[...]
    d = x.shape[1]
    h = _rms(x, g_ref[...]).astype(BF16)
    mixed = None
    for n, (b_ref, w_ref) in enumerate(((oa_ref, woa_ref), (gl_ref, wos_ref), (om_ref, wom_ref))):
        gate = jax.nn.sigmoid(jnp.dot(h, wg_ref[:, n * d:(n + 1) * d], preferred_element_type=F32))
        term = gate * jnp.dot(b_ref[...], w_ref[...], preferred_element_type=F32)
        mixed = term if mixed is None else mixed + term
    o_ref[...] = x + jnp.dot(mixed.astype(BF16), wout_ref[...], preferred_element_type=F32)


def _merge(x, oa, gl, om, gain, wg, woa, wos, wom, wout, l):
    rows, d = x.shape
    tile = lambda n: pl.BlockSpec((ROW_TILE, n), lambda i: (i, 0))
    return pl.pallas_call(
        _merge_kernel,
        grid=(rows // ROW_TILE,),
        in_specs=[tile(d), tile(oa.shape[1]), tile(gl.shape[1]), tile(om.shape[1])]
                 + [_layer(t, l) for t in (gain, wg, woa, wos, wom, wout)],
        out_specs=tile(d),
        out_shape=jax.ShapeDtypeStruct((rows, d), F32),
        compiler_params=_params("parallel"),
        name="merge",
    )(x, oa, gl, om, gain, wg, woa, wos, wom, wout)


def _ffn_kernel(x_ref, g_ref, w1_ref, w2_ref, gf_ref, o_ref, *, final_norm):
    x = x_ref[...]
    d = x.shape[1]
    h = _rms(x, g_ref[...]).astype(BF16)
    acc = x
    for c in range(w1_ref.shape[1] // d):
        a = jnp.maximum(jnp.dot(h, w1_ref[:, c * d:(c + 1) * d], preferred_element_type=F32), 0.0)
        acc = acc + jnp.dot((a * a).astype(BF16), w2_ref[c * d:(c + 1) * d, :], preferred_element_type=F32)
    o_ref[...] = _rms(acc, gf_ref[...]) if final_norm else acc


def _ffn(x, gain, w1, w2, gain_final, l, final_norm):
    rows, d = x.shape
    return pl.pallas_call(
        functools.partial(_ffn_kernel, final_norm=final_norm),
        grid=(rows // ROW_TILE,),
        in_specs=[pl.BlockSpec((ROW_TILE, d), lambda i: (i, 0)),
                  _layer(gain, l), _layer(w1, l), _layer(w2, l), _resident(gain_final)],
        out_specs=pl.BlockSpec((ROW_TILE, d), lambda i: (i, 0)),
        out_shape=jax.ShapeDtypeStruct((rows, d), F32),
        compiler_params=_params("parallel"),
        name="ffn",
    )(x, gain, w1, w2, gain_final)


def _alibi_slopes():
    n = 2 * N_HEADS
    return jnp.asarray(2.0 ** (-ALIBI_MAX_BIAS * np.arange(1, n + 1) / n), F32)


def _pair_cols(w):
    return w.reshape(w.shape[:-1] + (N_HEADS, HEAD_DIM))[..., PAIR_ORDER, :].reshape(w.shape)


def _pair_rows(w):
    return w.reshape(w.shape[:-2] + (N_HEADS, HEAD_DIM, w.shape[-1]))[..., PAIR_ORDER, :, :].reshape(w.shape)


def _swa_constants(slopes):
    w = SWA_WINDOW
    s_rel = jnp.arange(2 * w, dtype=F32) - w
    e = jnp.zeros((2 * w, 2 * LANES), F32)
    for o in (HEAD_DIM, LANES):
        e = e.at[:, o:o + 3].set(s_rel[:, None]).at[:, o + 3:o + 6].set(1.0)
    t_rel = jnp.arange(w, dtype=F32)
    sl = slopes * LOG2E
    aug = jnp.zeros((N_HEADS, HEAD_DIM, w), F32)
    for c, part in enumerate(_split3(sl)):
        aug = aug.at[:, c, :].set(part[:, None])
    for c, part in enumerate(_split3(-sl[:, None] * t_rel[None, :])):
        aug = aug.at[:, 3 + c, :].set(part)
    return e.astype(BF16), aug.reshape(2, N_PAIRS, HEAD_DIM, w)


def _moba_constants(slopes, seq):
    assert seq // MOBA_BLOCK <= MOBA_COLS
    pos = jnp.arange(seq)
    blk = pos // MOBA_BLOCK
    off = (pos % MOBA_BLOCK).astype(F32)
    base = (blk * MOBA_BLOCK).astype(F32)
    n = MOBA_COLS
    onehot = (blk[:, None] == jnp.arange(n)[None, :]).astype(F32)
    e = jnp.zeros((seq, LANES), F32)
    for o in (MOBA_HI, 0):
        e = e.at[:, o:o + n].set(onehot)
        e = e.at[:, o + n:o + n + 3].set(off[:, None]).at[:, o + n + 3:o + n + 6].set(base[:, None])
    aug = jnp.zeros((N_HEADS, MOBA_COLS, LANES), F32)
    for c, part in enumerate(_split3(slopes * LOG2E)):
        aug = aug.at[:, c, :].set(part[:, None]).at[:, 3 + c, :].set(part[:, None])
    return e.astype(BF16), aug


def _ssm_weights(lam_re, lam_im, log_step, b_re, b_im, c_re, c_im):
    lr, li = lam_re.astype(F32), lam_im.astype(F32)
    dt = jnp.exp(log_step.astype(F32))[:, None]
    mag = jnp.exp(lr * dt)
    ab_re, ab_im = mag * jnp.cos(li * dt), mag * jnp.sin(li * dt)
    den = lr * lr + li * li
    nr, ni = ab_re - 1.0, ab_im
    f_re, f_im = (nr * lr + ni * li) / den, (ni * lr - nr * li) / den
    br, bi = b_re.astype(F32), b_im.astype(F32)
    bb_re = f_re[..., None] * br - f_im[..., None] * bi
    bb_im = f_re[..., None] * bi + f_im[..., None] * br
    half = SSM_GROUPS // 2
    eye = jnp.eye(half, dtype=F32)

    def in_map(bb):
        return jnp.stack([jnp.einsum('gph,gk->ghkp', bb[s * half:(s + 1) * half], eye)
                          .reshape(half * SSM_GROUP, half * SSM_STATE) for s in range(2)])

    def out_map(c):
        return jnp.stack([jnp.einsum('ghp,gk->kpgh', c[s * half:(s + 1) * half], eye)
                          .reshape(half * SSM_STATE, half * SSM_GROUP) for s in range(2)])

    a = jnp.stack([ab_re.reshape(-1), ab_im.reshape(-1)])
    return (in_map(bb_re).astype(BF16), in_map(bb_im).astype(BF16), a,
            out_map(c_re.astype(F32)).astype(BF16), out_map(-c_im.astype(F32)).astype(BF16))


def kernel(x, norm_mix, w_in, sinks, lam_re, lam_im, log_step, b_re, b_im, c_re, c_im, d_skip, w_glu,
           w_o_swa, w_o_ssm, w_o_moba, w_out, norm_ffn, w_ff1, w_ff2, norm_final):
    batch, seq, d = x.shape
    depth = w_in.shape[0]
    assert batch * 2 == SUBLANES, "the S5 scan packs two time steps of all batches into one vreg"
    qw = N_HEADS * HEAD_DIM
    kw = qw // KV_GROUP
    slopes = _alibi_slopes()
    swa_e, swa_aug = _swa_constants(slopes[:N_HEADS])
    moba_e, moba_aug = _moba_constants(slopes[N_HEADS:], seq)

    o = np.cumsum((0, qw, kw, kw, SSM_WIDTH, qw, kw, kw))
    w_proj = jnp.concatenate([
        _pair_cols(w_in[..., o[0]:o[1]]) * (ATTN_SCALE * LOG2E), w_in[..., o[1]:o[4]],
        _pair_cols(w_in[..., o[4]:o[5]]) * (ATTN_SCALE * LOG2E), w_in[..., o[5]:o[7]]], axis=-1).astype(BF16)
    w_gate = w_in[..., o[7]:].astype(BF16)
    gain_mix = norm_mix.reshape(depth, 1, d)
    gain_ffn = norm_ffn.reshape(depth, 1, d)
    gain_final = norm_final.reshape(1, d)
    sink_tab = jnp.repeat(sinks.astype(F32) * LOG2E, SWA_WINDOW, axis=1).reshape(depth, 1, N_HEADS * SWA_WINDOW)
    wbr, wbi, a, cr, ci = jax.vmap(_ssm_weights)(lam_re, lam_im, log_step, b_re, b_im, c_re, c_im)
    skip = d_skip.astype(F32).reshape(depth, 1, SSM_WIDTH)
    wglu = w_glu.astype(BF16)
    woa, wos, wom = _pair_rows(w_o_swa).astype(BF16), w_o_ssm.astype(BF16), _pair_rows(w_o_moba).astype(BF16)
    wout, w1, w2 = w_out.astype(BF16), w_ff1.astype(BF16), w_ff2.astype(BF16)

    xs = x.reshape(batch * seq, d)
    for l in range(depth):
        qa, ka, va, u, qm, km, vm = _norm_proj(xs, gain_mix, w_proj, l)
        oa = _swa(qa, ka, va, swa_e, swa_aug, sink_tab, batch, l)
        gl = _ssm(u, wbr, wbi, a, cr, ci, skip, wglu, batch, l)
        om = _moba(qm, km, vm, moba_e, moba_aug, batch)
        xs = _merge(xs, oa, gl, om, gain_mix, w_gate, woa, wos, wom, wout, l)
        xs = _ffn(xs, gain_ffn, w1, w2, gain_final, l, final_norm=(l == depth - 1))
    return xs.reshape(batch, seq, d)
```

```python
import functools

import numpy as np
import jax
import jax.numpy as jnp
from jax import lax
from jax.experimental import pallas as pl
from jax.experimental.pallas import tpu as pltpu

F32 = jnp.float32
BF16 = jnp.bfloat16

HEAD_DIM = 64
N_HEADS = 8
KV_GROUP = 4
N_TILES = N_HEADS // 2
ATTN_SCALE = HEAD_DIM ** -0.5
LOG2E = 1.4426950408889634
SWA_WINDOW = 128
MOBA_BLOCK = 256
MOBA_TOPK = 3
SSM_GROUPS = 32
SSM_GROUP = 16
SSM_STATE = 64
SSM_WIDTH = SSM_GROUPS * SSM_GROUP
SSM_LANES = SSM_GROUPS * SSM_STATE
NORM_EPS = 1e-6
ALIBI_MAX_BIAS = 8.0
MASK_VALUE = -1e30
M_INIT = 0.5 * MASK_VALUE
GATE_FLOOR = -3.0e38

LANES = 128
SUBLANES = 8
BF16_ROWS = 16
VMEM_LIMIT = 56 * 1024 * 1024

ROW_TILE = 512
SSM_CHUNK = 128
SWA_TILE = 512
SCAN_LANES = 512
MOBA_COLS = 2 * SUBLANES
MOBA_HI = HEAD_DIM
VT_ROWS = HEAD_DIM + BF16_ROWS
VISIT = 2


def _params(*sem):
    return pltpu.CompilerParams(dimension_semantics=sem, vmem_limit_bytes=VMEM_LIMIT)


def _resident(arr):
    zeros = (0,) * arr.ndim
    return pl.BlockSpec(arr.shape, lambda *_: zeros)


def _layer(arr, l):
    zeros = (0,) * (arr.ndim - 1)
    return pl.BlockSpec((None,) + arr.shape[1:], lambda *_: (l,) + zeros)


def _rms(x, gain):
    return x * lax.rsqrt(jnp.mean(x * x, axis=-1, keepdims=True) + NORM_EPS) * gain


def _split3(x):
    hi = x.astype(BF16).astype(F32)
    mid = (x - hi).astype(BF16).astype(F32)
    lo = (x - hi - mid).astype(BF16).astype(F32)
    return hi, mid, lo


def _low_lanes():
    return lax.broadcasted_iota(jnp.int32, (1, LANES), 1) < HEAD_DIM


_PROJ_SIZES = (512, 128, 128, 512, 512, 128, 128)
_PROJ_WIDTH = sum(_PROJ_SIZES)
_Q_SCALE = ATTN_SCALE * LOG2E


def _norm_proj_kernel(x_ref, g_ref, w_ref, qa_ref, ka_ref, va_ref, u_ref, qm_ref, km_ref, vm_ref):
    h = _rms(x_ref[...], g_ref[...]).astype(BF16)
    proj = jnp.dot(h, w_ref[...], preferred_element_type=F32)
    start = 0
    for ref, size in zip((qa_ref, ka_ref, va_ref, u_ref, qm_ref, km_ref, vm_ref), _PROJ_SIZES):
        part = proj[:, start:start + size]
        if ref is qa_ref or ref is qm_ref:
            part = part * _Q_SCALE
        ref[...] = part.astype(ref.dtype)
        start += size


def _norm_proj(x, gain, w, l):
    rows, d = x.shape
    dts = (BF16, BF16, BF16, F32, BF16, BF16, BF16)
    return pl.pallas_call(
        _norm_proj_kernel,
        grid=(rows // ROW_TILE,),
        in_specs=[pl.BlockSpec((ROW_TILE, d), lambda i: (i, 0)), _layer(gain, l),
                  pl.BlockSpec((None, d, _PROJ_WIDTH), lambda i: (l, 0, 0))],
        out_specs=[pl.BlockSpec((ROW_TILE, n), lambda i: (i, 0)) for n in _PROJ_SIZES],
        out_shape=[jax.ShapeDtypeStruct((rows, n), dt) for n, dt in zip(_PROJ_SIZES, dts)],
        compiler_params=_params("parallel"),
        name="norm_proj",
    )(x, gain, w)


def _swa_kernel(q_ref, k_ref, v_ref, e_ref, aug_ref, sink_ref, o_ref, kc_ref, vt_ref, qt_ref, mb_ref):
    qi = pl.program_id(1)
    w = SWA_WINDOW
    half_cols = KV_GROUP * w
    low = _low_lanes()

    @pl.when(qi == 0)
    def _():
        k = k_ref[...]
        zero = jnp.zeros_like(k)
        kc_ref[:, 0:LANES] = jnp.where(low, k, zero)
        kc_ref[:, LANES:] = jnp.where(low, zero, k)
        vt = v_ref[...].astype(F32).T
        ones_row = jnp.where(lax.broadcasted_iota(jnp.int32, (BF16_ROWS, w), 0) == 0, 1.0, 0.0).astype(BF16)
        for n in range(vt_ref.shape[0]):
            for half in range(2):
                r0 = half * VT_ROWS
                vt_ref[n, r0:r0 + HEAD_DIM, :] = vt[half * HEAD_DIM:(half + 1) * HEAD_DIM,
                                                    n * w:(n + 1) * w].astype(BF16)
                vt_ref[n, r0 + HEAD_DIM:r0 + VT_ROWS, :] = ones_row
        qt_ref[...] = jnp.zeros(qt_ref.shape, BF16)
        key = lax.broadcasted_iota(jnp.int32, (2 * w, w), 0)
        qry = lax.broadcasted_iota(jnp.int32, (2 * w, w), 1)
        band = (key > qry) & (key <= qry + w)
        mb_ref[0] = jnp.where(band, 0.0, MASK_VALUE)
        mb_ref[1] = jnp.where(band & (key >= w), 0.0, MASK_VALUE)

    lane2 = lax.broadcasted_iota(jnp.int32, (1, 2 * LANES), 1)
    e_lanes = (lane2 >= HEAD_DIM) & (lane2 < 2 * LANES - HEAD_DIM)
    e_cols = e_ref[...]
    sink = sink_ref[...]
    for blk in range(SWA_TILE // w):
        t0 = qi * SWA_TILE + blk * w
        prev = jnp.maximum(t0 - w, 0)
        kwin = jnp.concatenate([kc_ref[pl.ds(pl.multiple_of(prev, w), w), :],
                                kc_ref[pl.ds(pl.multiple_of(t0, w), w), :]], axis=0)
        kcat = jnp.where(e_lanes, e_cols, kwin)
        vtw = jnp.concatenate([vt_ref[prev // w], vt_ref[t0 // w]], axis=1)
        for tile in range(N_TILES):
            qpt = q_ref[blk * w:(blk + 1) * w, tile * LANES:(tile + 1) * LANES].astype(F32).T
            for r in range(2):
                head = 2 * tile + r
                qh = qpt[r * HEAD_DIM:(r + 1) * HEAD_DIM]
                group = head // KV_GROUP
                rows = [qh, aug_ref[head]] if group == 0 else [aug_ref[head], qh]
                qt_ref[blk, group * LANES:(group + 1) * LANES, head * w:(head + 1) * w] = (
                    jnp.concatenate(rows, axis=0).astype(BF16))
        mb = mb_ref[(t0 == 0).astype(jnp.int32)]
        s = jnp.dot(kcat, qt_ref[blk], preferred_element_type=F32) + jnp.concatenate([mb] * N_HEADS, axis=1)
        m = jnp.maximum(jnp.max(s, axis=0, keepdims=True), sink)
        p = jnp.exp2((s - m).astype(BF16))
        pv = jnp.concatenate(
            [jnp.dot(vtw[half * VT_ROWS:(half + 1) * VT_ROWS], p[:, half * half_cols:(half + 1) * half_cols],
                     preferred_element_type=F32) for half in range(2)], axis=1)
        o_t = pv[0:HEAD_DIM] / (pv[HEAD_DIM:HEAD_DIM + 1] + jnp.exp2(sink - m))
        for tile in range(N_TILES):
            both = jnp.concatenate([o_t[:, (2 * tile + r) * w:(2 * tile + r + 1) * w] for r in range(2)], axis=0)
            o_ref[blk * w:(blk + 1) * w, tile * LANES:(tile + 1) * LANES] = both.T.astype(BF16)


def _swa(q, k, v, e_cols, aug, sinks, batch, l):
    seq = q.shape[0] // batch
    qw = N_HEADS * HEAD_DIM
    kw = k.shape[1]
    nq = seq // SWA_TILE
    nwin = SWA_TILE // SWA_WINDOW
    return pl.pallas_call(
        _swa_kernel,
        grid=(batch, nq),
        in_specs=[pl.BlockSpec((SWA_TILE, qw), lambda b, i: (b * nq + i, 0)),
                  pl.BlockSpec((seq, kw), lambda b, i: (b, 0)),
                  pl.BlockSpec((seq, kw), lambda b, i: (b, 0)),
                  _resident(e_cols), _resident(aug), _layer(sinks, l)],
        out_specs=pl.BlockSpec((SWA_TILE, qw), lambda b, i: (b * nq + i, 0)),
        out_shape=jax.ShapeDtypeStruct(q.shape, BF16),
        scratch_shapes=[pltpu.VMEM((seq, 2 * kw), BF16),
                        pltpu.VMEM((seq // SWA_WINDOW, 2 * VT_ROWS, SWA_WINDOW), BF16),
                        pltpu.VMEM((nwin, 2 * LANES, N_HEADS * SWA_WINDOW), BF16),
                        pltpu.VMEM((2, 2 * SWA_WINDOW, SWA_WINDOW), F32)],
        compiler_params=_params("parallel", "arbitrary"),
        name="swa_attention",
    )(q, k, v, e_cols, aug, sinks)


def _select_bias(gate, blk, j):
    g = jnp.where(blk < j, gate, GATE_FLOOR)
    bias = jnp.where(blk == j, 0.0, MASK_VALUE)
    blk_f = blk.astype(F32)
    for _ in range(MOBA_TOPK):
        mx = jnp.max(g, axis=0, keepdims=True)
        idx = jnp.min(jnp.where(g == mx, blk_f, float(MOBA_COLS)), axis=0, keepdims=True)
        hit = (blk_f == idx) & (mx > GATE_FLOOR)
        bias = jnp.where(hit, 0.0, bias)
        g = jnp.where(hit, GATE_FLOOR, g)
    return bias


def _moba_kernel(q_ref, k_ref, v_ref, e_ref, aug_ref, o_ref, kc_ref, vt_ref, km_ref, qt_ref, cb_ref, *, nblk):
    j = pl.program_id(1)
    L = MOBA_BLOCK
    cols = KV_GROUP * L
    low = _low_lanes()

    @pl.when(j == 0)
    def _():
        k = k_ref[...]
        e = e_ref[...]
        kc_ref[:, 0:LANES] = jnp.where(low, k, e)
        kc_ref[:, LANES:] = jnp.where(low, e, k)
        vt = v_ref[...].astype(F32).T
        ones_row = jnp.where(lax.broadcasted_iota(jnp.int32, (BF16_ROWS, VISIT * L), 0) == 0,
                             1.0, 0.0).astype(BF16)
        for n in range(nblk // VISIT):
            for half in range(2):
                r0 = half * VT_ROWS
                vt_ref[n, r0:r0 + HEAD_DIM, :] = vt[half * HEAD_DIM:(half + 1) * HEAD_DIM,
                                                    n * VISIT * L:(n + 1) * VISIT * L].astype(BF16)
                vt_ref[n, r0 + HEAD_DIM:r0 + VT_ROWS, :] = ones_row
        qt_ref[...] = jnp.zeros(qt_ref.shape, BF16)
        kmean = jnp.mean(k.astype(F32).reshape(nblk, L, LANES), axis=1)
        swapped = pltpu.roll(kmean, HEAD_DIM, axis=1)
        km_ref[...] = jnp.zeros(km_ref.shape, BF16)
        for group in range(2):
            placed = (jnp.where(low, kmean if group == 0 else swapped, 0.0),
                      jnp.where(low, 0.0, swapped if group == 0 else kmean))
            for r in range(2):
                for c, part in enumerate(_split3(placed[r])):
                    r0 = (3 * r + c) * BF16_ROWS
                    km_ref[group, r0:r0 + nblk, :] = part.astype(BF16)
        key = lax.broadcasted_iota(jnp.int32, (L, L), 0)
        qry = lax.broadcasted_iota(jnp.int32, (L, L), 1)
        cb_ref[0] = jnp.zeros((L, L), F32)
        cb_ref[1] = jnp.where(key <= qry, 0.0, MASK_VALUE)

    blk = lax.broadcasted_iota(jnp.int32, (MOBA_COLS, L), 0)
    for tile in range(N_TILES):
        group = (2 * tile) // KV_GROUP
        qpt = q_ref[:, tile * LANES:(tile + 1) * LANES].astype(F32).T
        g = jnp.dot(km_ref[group], qpt.astype(BF16), preferred_element_type=F32)
        for r in range(2):
            head = 2 * tile + r
            r0 = 3 * BF16_ROWS * r
            gate = (g[r0:r0 + BF16_ROWS] + g[r0 + BF16_ROWS:r0 + 2 * BF16_ROWS]
                    + g[r0 + 2 * BF16_ROWS:r0 + 3 * BF16_ROWS])
            bias = _select_bias(gate, blk, j)
            extra = jnp.concatenate([bias, jnp.concatenate([aug_ref[head]] * (L // LANES), axis=1),
                                     jnp.zeros((HEAD_DIM - 2 * MOBA_COLS, L), F32)], axis=0)
            qh = qpt[r * HEAD_DIM:(r + 1) * HEAD_DIM]
            rows = [qh, extra] if group == 0 else [extra, qh]
            qt_ref[group * LANES:(group + 1) * LANES, head * L:(head + 1) * L] = (
                jnp.concatenate(rows, axis=0).astype(BF16))

    def visit(n, carry):
        m, acc = carry
        cb = jnp.concatenate([cb_ref[(VISIT * n + i == j).astype(jnp.int32)] for i in range(VISIT)], axis=0)
        cb = jnp.concatenate([cb] * N_HEADS, axis=1)
        off = pl.multiple_of(n * (VISIT * L), VISIT * L)
        s = jnp.dot(kc_ref[pl.ds(off, VISIT * L), :], qt_ref[...], preferred_element_type=F32) + cb
        m_new = jnp.maximum(m, jnp.max(s, axis=0, keepdims=True))
        p = jnp.exp2((s - m_new).astype(BF16))
        pv = jnp.concatenate(
            [jnp.dot(vt_ref[n, half * VT_ROWS:(half + 1) * VT_ROWS, :], p[:, half * cols:(half + 1) * cols],
                     preferred_element_type=F32) for half in range(2)], axis=1)
        return m_new, jnp.exp2(m - m_new) * acc + pv

    init = (jnp.full((1, 2 * cols), M_INIT, F32), jnp.zeros((VT_ROWS, 2 * cols), F32))
    _, acc = lax.fori_loop(0, j // VISIT + 1, visit, init)
    for tile in range(N_TILES):
        parts = []
        for r in range(2):
            a = acc[:, (2 * tile + r) * L:(2 * tile + r + 1) * L]
            parts.append(a[0:HEAD_DIM] / a[HEAD_DIM:HEAD_DIM + 1])
        o_ref[:, tile * LANES:(tile + 1) * LANES] = jnp.concatenate(parts, axis=0).T.astype(BF16)


def _moba(q, k, v, e_cols, aug, batch):
    seq = q.shape[0] // batch
    nblk = seq // MOBA_BLOCK
    qw = N_HEADS * HEAD_DIM
    kw = k.shape[1]
    return pl.pallas_call(
        functools.partial(_moba_kernel, nblk=nblk),
        grid=(batch, nblk),
        in_specs=[pl.BlockSpec((MOBA_BLOCK, qw), lambda b, i: (b * nblk + i, 0)),
                  pl.BlockSpec((seq, kw), lambda b, i: (b, 0)),
                  pl.BlockSpec((seq, kw), lambda b, i: (b, 0)),
                  _resident(e_cols), _resident(aug)],
        out_specs=pl.BlockSpec((MOBA_BLOCK, qw), lambda b, i: (b * nblk + i, 0)),
        out_shape=jax.ShapeDtypeStruct(q.shape, BF16),
        scratch_shapes=[pltpu.VMEM((seq, 2 * kw), BF16),
                        pltpu.VMEM((nblk // VISIT, 2 * VT_ROWS, VISIT * MOBA_BLOCK), BF16),
                        pltpu.VMEM((2, 6 * BF16_ROWS, LANES), BF16),
                        pltpu.VMEM((2 * LANES, N_HEADS * MOBA_BLOCK), BF16),
                        pltpu.VMEM((2, MOBA_BLOCK, MOBA_BLOCK), F32)],
        compiler_params=_params("parallel", "arbitrary"),
        name="moba_attention",
    )(q, k, v, e_cols, aug)


def _ssm_kernel(u_ref, wbr_ref, wbi_ref, a_ref, cr_ref, ci_ref, d_ref, wglu_ref, o_ref,
                tb_ref, xr_ref, xi_ref, hr_ref, hi_ref):
    batch, steps, _ = u_ref.shape
    rows = batch * steps
    half_lanes = SSM_LANES // 2
    half_width = SSM_WIDTH // 2

    @pl.when(pl.program_id(0) == 0)
    def _():
        hr_ref[...] = jnp.zeros(hr_ref.shape, F32)
        hi_ref[...] = jnp.zeros(hi_ref.shape, F32)

    tiles = SSM_WIDTH // LANES
    for b in range(batch):
        ub_rows = u_ref[b]
        for c in range(tiles):
            tb_ref[c, pl.ds(b, steps, stride=batch), :] = ub_rows[:, c * LANES:(c + 1) * LANES]
    u = jnp.concatenate([tb_ref[c] for c in range(tiles)], axis=1)
    ub = u.astype(BF16)
    for s in range(2):
        us = ub[:, s * half_width:(s + 1) * half_width]
        xr_ref[:, s * half_lanes:(s + 1) * half_lanes] = jnp.dot(us, wbr_ref[s], preferred_element_type=F32)
        xi_ref[:, s * half_lanes:(s + 1) * half_lanes] = jnp.dot(us, wbi_ref[s], preferred_element_type=F32)

    top = lax.broadcasted_iota(jnp.int32, (SUBLANES, SCAN_LANES), 0) < batch
    for c in range(half_lanes // SCAN_LANES):
        lo = slice(c * SCAN_LANES, (c + 1) * SCAN_LANES)
        hi = slice(half_lanes + c * SCAN_LANES, half_lanes + (c + 1) * SCAN_LANES)
        ar = jnp.where(top, a_ref[0:1, lo], a_ref[0:1, hi])
        ai = jnp.where(top, a_ref[1:2, lo], a_ref[1:2, hi])

        def fold(x_lo, x_hi):
            return (jnp.where(top, x_lo, pltpu.roll(x_hi, batch, axis=0)),
                    jnp.where(top, pltpu.roll(x_lo, batch, axis=0), x_hi))

        def body(k, carry, lo=lo, hi=hi, ar=ar, ai=ai):
            pr, pi = carry
            rws = pl.ds(pl.multiple_of(k * SUBLANES, SUBLANES), SUBLANES)
            x1r, x2r = fold(xr_ref[rws, lo], xr_ref[rws, hi])
            x1i, x2i = fold(xi_ref[rws, lo], xi_ref[rws, hi])
            h1r = ar * pr - ai * pi + x1r
            h1i = ar * pi + ai * pr + x1i
            h2r = ar * h1r - ai * h1i + x2r
            h2i = ar * h1i + ai * h1r + x2i
            xr_ref[rws, lo], xr_ref[rws, hi] = fold(h1r, h2r)
            xi_ref[rws, lo], xi_ref[rws, hi] = fold(h1i, h2i)
            return h2r, h2i

        pr, pi = lax.fori_loop(0, rows // SUBLANES, body, (hr_ref[:, lo], hi_ref[:, lo]))
        hr_ref[:, lo] = pr
        hi_ref[:, lo] = pi

    ys = []
    for s in range(2):
        hr = xr_ref[:, s * half_lanes:(s + 1) * half_lanes].astype(BF16)
        hi = xi_ref[:, s * half_lanes:(s + 1) * half_lanes].astype(BF16)
        ys.append(jnp.dot(hr, cr_ref[s], preferred_element_type=F32)
                  + jnp.dot(hi, ci_ref[s], preferred_element_type=F32))
    y = jnp.concatenate(ys, axis=1) + d_ref[...] * u
    z = jnp.dot(jax.nn.gelu(y).astype(BF16), wglu_ref[...], preferred_element_type=F32)
    out = z[:, :SSM_WIDTH] * jax.nn.sigmoid(z[:, SSM_WIDTH:])
    for c in range(tiles):
        tb_ref[c] = out[:, c * LANES:(c + 1) * LANES]
    for b in range(batch):
        o_ref[b] = jnp.concatenate([tb_ref[c, pl.ds(b, steps, stride=batch), :] for c in range(tiles)],
                                   axis=1).astype(BF16)


def _ssm(u, wbr, wbi, a, cr, ci, d, wglu, batch, l):
    seq = u.shape[0] // batch
    rows = SSM_CHUNK * batch
    block = pl.BlockSpec((batch, SSM_CHUNK, SSM_WIDTH), lambda i: (0, i, 0))
    out = pl.pallas_call(
        _ssm_kernel,
        grid=(seq // SSM_CHUNK,),
        in_specs=[block] + [_layer(t, l) for t in (wbr, wbi, a, cr, ci, d, wglu)],
        out_specs=block,
        out_shape=jax.ShapeDtypeStruct((batch, seq, SSM_WIDTH), BF16),
        scratch_shapes=[pltpu.VMEM((SSM_WIDTH // LANES, rows, LANES), F32),
                        pltpu.VMEM((rows, SSM_LANES), F32), pltpu.VMEM((rows, SSM_LANES), F32),
                        pltpu.VMEM((SUBLANES, SSM_LANES // 2), F32), pltpu.VMEM((SUBLANES, SSM_LANES // 2), F32)],
        compiler_params=_params("arbitrary"),
        name="s5_ssm",
    )(u.reshape(batch, seq, SSM_WIDTH), wbr, wbi, a, cr, ci, d, wglu)
    return out.reshape(batch * seq, SSM_WIDTH)


def _merge_kernel(x_ref, oa_ref, gl_ref, om_ref, g_ref, wga_ref, wgs_ref, wgm_ref, woa_ref, wos_ref, wom_ref,
                  wout_ref, o_ref):
    x = x_ref[...]
    h = _rms(x, g_ref[...]).astype(BF16)
    mixed = None
    for b_ref, wg_ref, w_ref in ((oa_ref, wga_ref, woa_ref), (gl_ref, wgs_ref, wos_ref), (om_ref, wgm_ref, wom_ref)):
        gate = jax.nn.sigmoid(jnp.dot(h, wg_ref[...], preferred_element_type=F32))
        term = gate * jnp.dot(b_ref[...], w_ref[...], preferred_element_type=F32)
        mixed = term if mixed is None else mixed + term
    o_ref[...] = x + jnp.dot(mixed.astype(BF16), wout_ref[...], preferred_element_type=F32)


def _merge(x, oa, gl, om, gain, w_in, woa, wos, wom, wout, l):
    rows, d = x.shape
    tile = lambda n: pl.BlockSpec((ROW_TILE, n), lambda i: (i, 0))
    first = _PROJ_WIDTH // d
    gates = [pl.BlockSpec((None, d, d), lambda i, n=n: (l, 0, first + n)) for n in range(3)]
    return pl.pallas_call(
        _merge_kernel,
        grid=(rows // ROW_TILE,),
        in_specs=[tile(d), tile(oa.shape[1]), tile(gl.shape[1]), tile(om.shape[1]), _layer(gain, l)] + gates
                 + [_layer(t, l) for t in (woa, wos, wom, wout)],
        out_specs=tile(d),
        out_shape=jax.ShapeDtypeStruct((rows, d), F32),
        compiler_params=_params("parallel"),
        name="merge",
    )(x, oa, gl, om, gain, w_in, w_in, w_in, woa, wos, wom, wout)


def _ffn_kernel(x_ref, g_ref, w1_ref, w2_ref, gf_ref, o_ref, *, final_norm):
    x = x_ref[...]
    d = x.shape[1]
    h = _rms(x, g_ref[...]).astype(BF16)
    acc = x
    for c in range(w1_ref.shape[1] // d):
        a = jnp.maximum(jnp.dot(h, w1_ref[:, c * d:(c + 1) * d], preferred_element_type=F32), 0.0)
        acc = acc + jnp.dot((a * a).astype(BF16), w2_ref[c * d:(c + 1) * d, :], preferred_element_type=F32)
    o_ref[...] = _rms(acc, gf_ref[...]) if final_norm else acc


def _ffn(x, gain, w1, w2, gain_final, l, final_norm):
    rows, d = x.shape
    return pl.pallas_call(
        functools.partial(_ffn_kernel, final_norm=final_norm),
        grid=(rows // ROW_TILE,),
        in_specs=[pl.BlockSpec((ROW_TILE, d), lambda i: (i, 0)),
                  _layer(gain, l), _layer(w1, l), _layer(w2, l), _resident(gain_final)],
        out_specs=pl.BlockSpec((ROW_TILE, d), lambda i: (i, 0)),
        out_shape=jax.ShapeDtypeStruct((rows, d), F32),
        compiler_params=_params("parallel"),
        name="ffn",
    )(x, gain, w1, w2, gain_final)


def _alibi_slopes():
    n = 2 * N_HEADS
    return jnp.asarray(2.0 ** (-ALIBI_MAX_BIAS * np.arange(1, n + 1) / n), F32)


def _swa_constants(slopes):
    w = SWA_WINDOW
    s_rel = jnp.arange(2 * w, dtype=F32) - w
    e = jnp.zeros((2 * w, 2 * LANES), F32)
    for o in (HEAD_DIM, LANES):
        e = e.at[:, o:o + 3].set(s_rel[:, None]).at[:, o + 3:o + 6].set(1.0)
    t_rel = jnp.arange(w, dtype=F32)
    sl = slopes * LOG2E
    aug = jnp.zeros((N_HEADS, HEAD_DIM, w), F32)
    for c, part in enumerate(_split3(sl)):
        aug = aug.at[:, c, :].set(part[:, None])
    for c, part in enumerate(_split3(-sl[:, None] * t_rel[None, :])):
        aug = aug.at[:, 3 + c, :].set(part)
    return e.astype(BF16), aug


def _moba_constants(slopes, seq):
    assert seq // MOBA_BLOCK <= MOBA_COLS and (seq // MOBA_BLOCK) % VISIT == 0
    pos = jnp.arange(seq)
    blk = pos // MOBA_BLOCK
    off = (pos % MOBA_BLOCK).astype(F32)
    base = (blk * MOBA_BLOCK).astype(F32)
    n = MOBA_COLS
    onehot = (blk[:, None] == jnp.arange(n)[None, :]).astype(F32)
    e = jnp.zeros((seq, LANES), F32)
    for o in (MOBA_HI, 0):
        e = e.at[:, o:o + n].set(onehot)
        e = e.at[:, o + n:o + n + 3].set(off[:, None]).at[:, o + n + 3:o + n + 6].set(base[:, None])
    aug = jnp.zeros((N_HEADS, MOBA_COLS, LANES), F32)
    for c, part in enumerate(_split3(slopes * LOG2E)):
        aug = aug.at[:, c, :].set(part[:, None]).at[:, 3 + c, :].set(part[:, None])
    return e.astype(BF16), aug


def _ssm_weights(lam_re, lam_im, log_step, b_re, b_im, c_re, c_im):
    lr, li = lam_re.astype(F32), lam_im.astype(F32)
    dt = jnp.exp(log_step.astype(F32))[:, None]
    mag = jnp.exp(lr * dt)
    ab_re, ab_im = mag * jnp.cos(li * dt), mag * jnp.sin(li * dt)
    den = lr * lr + li * li
    nr, ni = ab_re - 1.0, ab_im
    f_re, f_im = (nr * lr + ni * li) / den, (ni * lr - nr * li) / den
    br, bi = b_re.astype(F32), b_im.astype(F32)
    bb_re = f_re[..., None] * br - f_im[..., None] * bi
    bb_im = f_re[..., None] * bi + f_im[..., None] * br
    half = SSM_GROUPS // 2
    eye = jnp.eye(half, dtype=F32)

    def in_map(bb):
        return jnp.stack([jnp.einsum('gph,gk->ghkp', bb[s * half:(s + 1) * half], eye)
                          .reshape(half * SSM_GROUP, half * SSM_STATE) for s in range(2)])

    def out_map(c):
        return jnp.stack([jnp.einsum('ghp,gk->kpgh', c[s * half:(s + 1) * half], eye)
                          .reshape(half * SSM_STATE, half * SSM_GROUP) for s in range(2)])

    a = jnp.stack([ab_re.reshape(-1), ab_im.reshape(-1)])
    return (in_map(bb_re).astype(BF16), in_map(bb_im).astype(BF16), a,
            out_map(c_re.astype(F32)).astype(BF16), out_map(-c_im.astype(F32)).astype(BF16))


def kernel(x, norm_mix, w_in, sinks, lam_re, lam_im, log_step, b_re, b_im, c_re, c_im, d_skip, w_glu,
           w_o_swa, w_o_ssm, w_o_moba, w_out, norm_ffn, w_ff1, w_ff2, norm_final):
    batch, seq, d = x.shape
    depth = w_in.shape[0]
    assert batch * 2 == SUBLANES, "the S5 scan packs two time steps of all batches into one vreg"
    slopes = _alibi_slopes()
    swa_e, swa_aug = _swa_constants(slopes[:N_HEADS])
    moba_e, moba_aug = _moba_constants(slopes[N_HEADS:], seq)

    w_all = w_in.astype(BF16)
    gain_mix = norm_mix.reshape(depth, 1, d)
    gain_ffn = norm_ffn.reshape(depth, 1, d)
    gain_final = norm_final.reshape(1, d)
    sink_tab = jnp.repeat(sinks.astype(F32) * LOG2E, SWA_WINDOW, axis=1).reshape(depth, 1, N_HEADS * SWA_WINDOW)
    wbr, wbi, a, cr, ci = jax.vmap(_ssm_weights)(lam_re, lam_im, log_step, b_re, b_im, c_re, c_im)
    skip = d_skip.astype(F32).reshape(depth, 1, SSM_WIDTH)
    wglu = w_glu.astype(BF16)
    woa, wos, wom = w_o_swa.astype(BF16), w_o_ssm.astype(BF16), w_o_moba.astype(BF16)
    wout, w1, w2 = w_out.astype(BF16), w_ff1.astype(BF16), w_ff2.astype(BF16)

    xs = x.reshape(batch * seq, d)
    for l in range(depth):
        qa, ka, va, u, qm, km, vm = _norm_proj(xs, gain_mix, w_all, l)
        oa = _swa(qa, ka, va, swa_e, swa_aug, sink_tab, batch, l)
        gl = _ssm(u, wbr, wbi, a, cr, ci, skip, wglu, batch, l)
        om = _moba(qm, km, vm, moba_e, moba_aug, batch)
        xs = _merge(xs, oa, gl, om, gain_mix, w_all, woa, wos, wom, wout, l)
        xs = _ffn(xs, gain_ffn, w1, w2, gain_final, l, final_norm=(l == depth - 1))
    return xs.reshape(batch, seq, d)
```

```python
import functools

import numpy as np
import jax
import jax.numpy as jnp
from jax import lax
from jax.experimental import pallas as pl
from jax.experimental.pallas import tpu as pltpu

F32 = jnp.float32
BF16 = jnp.bfloat16

HEAD_DIM = 64
N_HEADS = 8
KV_GROUP = 4
N_TILES = N_HEADS // 2
ATTN_SCALE = HEAD_DIM ** -0.5
LOG2E = 1.4426950408889634
SWA_WINDOW = 128
MOBA_BLOCK = 256
MOBA_TOPK = 3
SSM_GROUPS = 32
SSM_GROUP = 16
SSM_STATE = 64
SSM_WIDTH = SSM_GROUPS * SSM_GROUP
SSM_LANES = SSM_GROUPS * SSM_STATE
NORM_EPS = 1e-6
ALIBI_MAX_BIAS = 8.0
MASK_VALUE = -1e30
M_INIT = 0.5 * MASK_VALUE
GATE_FLOOR = -3.0e38

LANES = 128
SUBLANES = 8
BF16_ROWS = 16
VMEM_LIMIT = 56 * 1024 * 1024

ROW_TILE = 1024
SSM_CHUNK = 128
SWA_TILE = 512
SCAN_LANES = 512
MOBA_COLS = 2 * SUBLANES
MOBA_HI = HEAD_DIM
VT_ROWS = HEAD_DIM + BF16_ROWS
VISIT = 2


def _params(*sem):
    return pltpu.CompilerParams(dimension_semantics=sem, vmem_limit_bytes=VMEM_LIMIT)


def _resident(arr):
    zeros = (0,) * arr.ndim
    return pl.BlockSpec(arr.shape, lambda *_: zeros)


def _layer(arr, l):
    zeros = (0,) * (arr.ndim - 1)
    return pl.BlockSpec((None,) + arr.shape[1:], lambda *_: (l,) + zeros)


def _rms(x, gain):
    return x * lax.rsqrt(jnp.mean(x * x, axis=-1, keepdims=True) + NORM_EPS) * gain


def _split3(x):
    hi = x.astype(BF16).astype(F32)
    mid = (x - hi).astype(BF16).astype(F32)
    lo = (x - hi - mid).astype(BF16).astype(F32)
    return hi, mid, lo


def _low_lanes():
    return lax.broadcasted_iota(jnp.int32, (1, LANES), 1) < HEAD_DIM


_PROJ_SIZES = (512, 128, 128, 512, 512, 128, 128)
_PROJ_WIDTH = sum(_PROJ_SIZES)
_Q_SCALE = ATTN_SCALE * LOG2E


def _norm_proj_kernel(x_ref, g_ref, w_ref, qa_ref, ka_ref, va_ref, u_ref, qm_ref, km_ref, vm_ref):
    h = _rms(x_ref[...], g_ref[...]).astype(BF16)
    proj = jnp.dot(h, w_ref[...], preferred_element_type=F32)
    start = 0
    for ref, size in zip((qa_ref, ka_ref, va_ref, u_ref, qm_ref, km_ref, vm_ref), _PROJ_SIZES):
        part = proj[:, start:start + size]
        if ref is qa_ref or ref is qm_ref:
            part = part * _Q_SCALE
        ref[...] = part.astype(ref.dtype)
        start += size


def _norm_proj(x, gain, w, l):
    rows, d = x.shape
    dts = (BF16, BF16, BF16, F32, BF16, BF16, BF16)
    return pl.pallas_call(
        _norm_proj_kernel,
        grid=(rows // ROW_TILE,),
        in_specs=[pl.BlockSpec((ROW_TILE, d), lambda i: (i, 0)), _layer(gain, l),
                  pl.BlockSpec((None, d, _PROJ_WIDTH), lambda i: (l, 0, 0))],
        out_specs=[pl.BlockSpec((ROW_TILE, n), lambda i: (i, 0)) for n in _PROJ_SIZES],
        out_shape=[jax.ShapeDtypeStruct((rows, n), dt) for n, dt in zip(_PROJ_SIZES, dts)],
        compiler_params=_params("parallel"),
        name="norm_proj",
    )(x, gain, w)


def _swa_kernel(q_ref, k_ref, v_ref, e_ref, aug_ref, sink_ref, o_ref, kc_ref, vt_ref, qt_ref, mb_ref):
    qi = pl.program_id(1)
    w = SWA_WINDOW
    half_cols = KV_GROUP * w
    low = _low_lanes()

    @pl.when(qi == 0)
    def _():
        k = k_ref[...]
        zero = jnp.zeros_like(k)
        kc_ref[:, 0:LANES] = jnp.where(low, k, zero)
        kc_ref[:, LANES:] = jnp.where(low, zero, k)
        vt = v_ref[...].astype(F32).T
        ones_row = jnp.where(lax.broadcasted_iota(jnp.int32, (BF16_ROWS, w), 0) == 0, 1.0, 0.0).astype(BF16)
        for n in range(vt_ref.shape[0]):
            for half in range(2):
                r0 = half * VT_ROWS
                vt_ref[n, r0:r0 + HEAD_DIM, :] = vt[half * HEAD_DIM:(half + 1) * HEAD_DIM,
                                                    n * w:(n + 1) * w].astype(BF16)
                vt_ref[n, r0 + HEAD_DIM:r0 + VT_ROWS, :] = ones_row
        qt_ref[...] = jnp.zeros(qt_ref.shape, BF16)
        key = lax.broadcasted_iota(jnp.int32, (2 * w, w), 0)
        qry = lax.broadcasted_iota(jnp.int32, (2 * w, w), 1)
        band = (key > qry) & (key <= qry + w)
        mb_ref[0] = jnp.where(band, 0.0, MASK_VALUE)
        mb_ref[1] = jnp.where(band & (key >= w), 0.0, MASK_VALUE)

    lane2 = lax.broadcasted_iota(jnp.int32, (1, 2 * LANES), 1)
    e_lanes = (lane2 >= HEAD_DIM) & (lane2 < 2 * LANES - HEAD_DIM)
    e_cols = e_ref[...]
    sink = sink_ref[...]
    for blk in range(SWA_TILE // w):
        t0 = qi * SWA_TILE + blk * w
        prev = jnp.maximum(t0 - w, 0)
        kwin = jnp.concatenate([kc_ref[pl.ds(pl.multiple_of(prev, w), w), :],
                                kc_ref[pl.ds(pl.multiple_of(t0, w), w), :]], axis=0)
        kcat = jnp.where(e_lanes, e_cols, kwin)
        vtw = jnp.concatenate([vt_ref[prev // w], vt_ref[t0 // w]], axis=1)
        for tile in range(N_TILES):
            qpt = q_ref[blk * w:(blk + 1) * w, tile * LANES:(tile + 1) * LANES].astype(F32).T
            for r in range(2):
                head = 2 * tile + r
                qh = qpt[r * HEAD_DIM:(r + 1) * HEAD_DIM]
                group = head // KV_GROUP
                rows = [qh, aug_ref[head]] if group == 0 else [aug_ref[head], qh]
                qt_ref[blk, group * LANES:(group + 1) * LANES, head * w:(head + 1) * w] = (
                    jnp.concatenate(rows, axis=0).astype(BF16))
        mb = mb_ref[(t0 == 0).astype(jnp.int32)]
        s = jnp.dot(kcat, qt_ref[blk], preferred_element_type=F32) + jnp.concatenate([mb] * N_HEADS, axis=1)
        m = jnp.maximum(jnp.max(s, axis=0, keepdims=True), sink)
        p = jnp.exp2((s - m).astype(BF16))
        pv = jnp.concatenate(
            [jnp.dot(vtw[half * VT_ROWS:(half + 1) * VT_ROWS], p[:, half * half_cols:(half + 1) * half_cols],
                     preferred_element_type=F32) for half in range(2)], axis=1)
        o_t = pv[0:HEAD_DIM] / (pv[HEAD_DIM:HEAD_DIM + 1] + jnp.exp2(sink - m))
        for tile in range(N_TILES):
            both = jnp.concatenate([o_t[:, (2 * tile + r) * w:(2 * tile + r + 1) * w] for r in range(2)], axis=0)
            o_ref[blk * w:(blk + 1) * w, tile * LANES:(tile + 1) * LANES] = both.T.astype(BF16)


def _swa(q, k, v, e_cols, aug, sinks, batch, l):
    seq = q.shape[0] // batch
    qw = N_HEADS * HEAD_DIM
    kw = k.shape[1]
    nq = seq // SWA_TILE
    nwin = SWA_TILE // SWA_WINDOW
    return pl.pallas_call(
        _swa_kernel,
        grid=(batch, nq),
        in_specs=[pl.BlockSpec((SWA_TILE, qw), lambda b, i: (b * nq + i, 0)),
                  pl.BlockSpec((seq, kw), lambda b, i: (b, 0)),
                  pl.BlockSpec((seq, kw), lambda b, i: (b, 0)),
                  _resident(e_cols), _resident(aug), _layer(sinks, l)],
        out_specs=pl.BlockSpec((SWA_TILE, qw), lambda b, i: (b * nq + i, 0)),
        out_shape=jax.ShapeDtypeStruct(q.shape, BF16),
        scratch_shapes=[pltpu.VMEM((seq, 2 * kw), BF16),
                        pltpu.VMEM((seq // SWA_WINDOW, 2 * VT_ROWS, SWA_WINDOW), BF16),
                        pltpu.VMEM((nwin, 2 * LANES, N_HEADS * SWA_WINDOW), BF16),
                        pltpu.VMEM((2, 2 * SWA_WINDOW, SWA_WINDOW), F32)],
        compiler_params=_params("parallel", "arbitrary"),
        name="swa_attention",
    )(q, k, v, e_cols, aug, sinks)


def _select_bias(gate, blk, j):
    g = jnp.where(blk < j, gate, GATE_FLOOR)
    bias = jnp.where(blk == j, 0.0, MASK_VALUE)
    blk_f = blk.astype(F32)
    for _ in range(MOBA_TOPK):
        mx = jnp.max(g, axis=0, keepdims=True)
        idx = jnp.min(jnp.where(g == mx, blk_f, float(MOBA_COLS)), axis=0, keepdims=True)
        hit = (blk_f == idx) & (mx > GATE_FLOOR)
        bias = jnp.where(hit, 0.0, bias)
        g = jnp.where(hit, GATE_FLOOR, g)
    return bias


def _moba_kernel(q_ref, k_ref, v_ref, e_ref, aug_ref, o_ref, kc_ref, vt_ref, km_ref, qt_ref, cb_ref, *, nblk):
    j = pl.program_id(1)
    L = MOBA_BLOCK
    cols = KV_GROUP * L
    low = _low_lanes()

    @pl.when(j == 0)
    def _():
        k = k_ref[...]
        e = e_ref[...]
        kc_ref[:, 0:LANES] = jnp.where(low, k, e)
        kc_ref[:, LANES:] = jnp.where(low, e, k)
        vt = v_ref[...].astype(F32).T
        ones_row = jnp.where(lax.broadcasted_iota(jnp.int32, (BF16_ROWS, VISIT * L), 0) == 0,
                             1.0, 0.0).astype(BF16)
        for n in range(nblk // VISIT):
            for half in range(2):
                r0 = half * VT_ROWS
                vt_ref[n, r0:r0 + HEAD_DIM, :] = vt[half * HEAD_DIM:(half + 1) * HEAD_DIM,
                                                    n * VISIT * L:(n + 1) * VISIT * L].astype(BF16)
                vt_ref[n, r0 + HEAD_DIM:r0 + VT_ROWS, :] = ones_row
        qt_ref[...] = jnp.zeros(qt_ref.shape, BF16)
        kmean = jnp.mean(k.astype(F32).reshape(nblk, L, LANES), axis=1)
        swapped = pltpu.roll(kmean, HEAD_DIM, axis=1)
        km_ref[...] = jnp.zeros(km_ref.shape, BF16)
        for group in range(2):
            placed = (jnp.where(low, kmean if group == 0 else swapped, 0.0),
                      jnp.where(low, 0.0, swapped if group == 0 else kmean))
            for r in range(2):
                for c, part in enumerate(_split3(placed[r])):
                    r0 = (3 * r + c) * BF16_ROWS
                    km_ref[group, r0:r0 + nblk, :] = part.astype(BF16)
        key = lax.broadcasted_iota(jnp.int32, (L, L), 0)
        qry = lax.broadcasted_iota(jnp.int32, (L, L), 1)
        cb_ref[0] = jnp.zeros((L, L), F32)
        cb_ref[1] = jnp.where(key <= qry, 0.0, MASK_VALUE)

    qpts, gates = [], []
    for tile in range(N_TILES):
        group = (2 * tile) // KV_GROUP
        qpt = q_ref[:, tile * LANES:(tile + 1) * LANES].astype(F32).T
        g = jnp.dot(km_ref[group], qpt.astype(BF16), preferred_element_type=F32)
        for r in range(2):
            r0 = 3 * BF16_ROWS * r
            gates.append(g[r0:r0 + BF16_ROWS] + g[r0 + BF16_ROWS:r0 + 2 * BF16_ROWS]
                         + g[r0 + 2 * BF16_ROWS:r0 + 3 * BF16_ROWS])
        qpts.append(qpt)
    blk = lax.broadcasted_iota(jnp.int32, (MOBA_COLS, N_HEADS * L), 0)
    bias_all = _select_bias(jnp.concatenate(gates, axis=1), blk, j)
    for head in range(N_HEADS):
        group = head // KV_GROUP
        extra = jnp.concatenate([bias_all[:, head * L:(head + 1) * L],
                                 jnp.concatenate([aug_ref[head]] * (L // LANES), axis=1),
                                 jnp.zeros((HEAD_DIM - 2 * MOBA_COLS, L), F32)], axis=0)
        qh = qpts[head // 2][(head % 2) * HEAD_DIM:(head % 2 + 1) * HEAD_DIM]
        rows = [qh, extra] if group == 0 else [extra, qh]
        qt_ref[group * LANES:(group + 1) * LANES, head * L:(head + 1) * L] = (
            jnp.concatenate(rows, axis=0).astype(BF16))

    def visit(n, count, carry):
        m, acc = carry
        cb = jnp.concatenate([cb_ref[(VISIT * n + i == j).astype(jnp.int32)] for i in range(count)], axis=0)
        cb = jnp.concatenate([cb] * N_HEADS, axis=1)
        off = pl.multiple_of(n * (VISIT * L), VISIT * L)
        s = jnp.dot(kc_ref[pl.ds(off, count * L), :], qt_ref[...], preferred_element_type=F32) + cb
        m_new = jnp.maximum(m, jnp.max(s, axis=0, keepdims=True))
        p = jnp.exp2((s - m_new).astype(BF16))
        pv = jnp.concatenate(
            [jnp.dot(vt_ref[n, half * VT_ROWS:(half + 1) * VT_ROWS, 0:count * L], p[:, half * cols:(half + 1) * cols],
                     preferred_element_type=F32) for half in range(2)], axis=1)
        return m_new, jnp.exp2(m - m_new) * acc + pv

    init = (jnp.full((1, 2 * cols), M_INIT, F32), jnp.zeros((VT_ROWS, 2 * cols), F32))
    full = (j + 1) // VISIT
    carry = lax.fori_loop(0, full, lambda n, c: visit(n, VISIT, c), init)
    _, acc = lax.cond((j + 1) % VISIT == 1, lambda c: visit(full, 1, c), lambda c: c, carry)
    for tile in range(N_TILES):
        parts = []
        for r in range(2):
            a = acc[:, (2 * tile + r) * L:(2 * tile + r + 1) * L]
            parts.append(a[0:HEAD_DIM] / a[HEAD_DIM:HEAD_DIM + 1])
        o_ref[:, tile * LANES:(tile + 1) * LANES] = jnp.concatenate(parts, axis=0).T.astype(BF16)


def _moba(q, k, v, e_cols, aug, batch):
    seq = q.shape[0] // batch
    nblk = seq // MOBA_BLOCK
    qw = N_HEADS * HEAD_DIM
    kw = k.shape[1]
    return pl.pallas_call(
        functools.partial(_moba_kernel, nblk=nblk),
        grid=(batch, nblk),
        in_specs=[pl.BlockSpec((MOBA_BLOCK, qw), lambda b, i: (b * nblk + i, 0)),
                  pl.BlockSpec((seq, kw), lambda b, i: (b, 0)),
                  pl.BlockSpec((seq, kw), lambda b, i: (b, 0)),
                  _resident(e_cols), _resident(aug)],
        out_specs=pl.BlockSpec((MOBA_BLOCK, qw), lambda b, i: (b * nblk + i, 0)),
        out_shape=jax.ShapeDtypeStruct(q.shape, BF16),
        scratch_shapes=[pltpu.VMEM((seq, 2 * kw), BF16),
                        pltpu.VMEM((nblk // VISIT, 2 * VT_ROWS, VISIT * MOBA_BLOCK), BF16),
                        pltpu.VMEM((2, 6 * BF16_ROWS, LANES), BF16),
                        pltpu.VMEM((2 * LANES, N_HEADS * MOBA_BLOCK), BF16),
                        pltpu.VMEM((2, MOBA_BLOCK, MOBA_BLOCK), F32)],
        compiler_params=_params("parallel", "arbitrary"),
        name="moba_attention",
    )(q, k, v, e_cols, aug)


def _ssm_kernel(u_ref, wbr_ref, wbi_ref, a_ref, cr_ref, ci_ref, d_ref, wglu_ref, o_ref,
                tb_ref, xr_ref, xi_ref, hr_ref, hi_ref):
    batch, steps, _ = u_ref.shape
    rows = batch * steps
    half_lanes = SSM_LANES // 2
    half_width = SSM_WIDTH // 2

    @pl.when(pl.program_id(0) == 0)
    def _():
        hr_ref[...] = jnp.zeros(hr_ref.shape, F32)
        hi_ref[...] = jnp.zeros(hi_ref.shape, F32)

    tiles = SSM_WIDTH // LANES
    for b in range(batch):
        ub_rows = u_ref[b]
        for c in range(tiles):
            tb_ref[c, pl.ds(b, steps, stride=batch), :] = ub_rows[:, c * LANES:(c + 1) * LANES]
    u = jnp.concatenate([tb_ref[c] for c in range(tiles)], axis=1)
    ub = u.astype(BF16)
    for s in range(2):
        us = ub[:, s * half_width:(s + 1) * half_width]
        xr_ref[:, s * half_lanes:(s + 1) * half_lanes] = jnp.dot(us, wbr_ref[s], preferred_element_type=F32)
        xi_ref[:, s * half_lanes:(s + 1) * half_lanes] = jnp.dot(us, wbi_ref[s], preferred_element_type=F32)

    top = lax.broadcasted_iota(jnp.int32, (SUBLANES, SCAN_LANES), 0) < batch
    for c in range(half_lanes // SCAN_LANES):
        lo = slice(c * SCAN_LANES, (c + 1) * SCAN_LANES)
        hi = slice(half_lanes + c * SCAN_LANES, half_lanes + (c + 1) * SCAN_LANES)
        ar = jnp.where(top, a_ref[0:1, lo], a_ref[0:1, hi])
        ai = jnp.where(top, a_ref[1:2, lo], a_ref[1:2, hi])

        def fold(x_lo, x_hi):
            return (jnp.where(top, x_lo, pltpu.roll(x_hi, batch, axis=0)),
                    jnp.where(top, pltpu.roll(x_lo, batch, axis=0), x_hi))

        def body(k, carry, lo=lo, hi=hi, ar=ar, ai=ai):
            pr, pi = carry
            rws = pl.ds(pl.multiple_of(k * SUBLANES, SUBLANES), SUBLANES)
            x1r, x2r = fold(xr_ref[rws, lo], xr_ref[rws, hi])
            x1i, x2i = fold(xi_ref[rws, lo], xi_ref[rws, hi])
            h1r = ar * pr - ai * pi + x1r
            h1i = ar * pi + ai * pr + x1i
            h2r = ar * h1r - ai * h1i + x2r
            h2i = ar * h1i + ai * h1r + x2i
            xr_ref[rws, lo], xr_ref[rws, hi] = fold(h1r, h2r)
            xi_ref[rws, lo], xi_ref[rws, hi] = fold(h1i, h2i)
            return h2r, h2i

        pr, pi = lax.fori_loop(0, rows // SUBLANES, body, (hr_ref[:, lo], hi_ref[:, lo]))
        hr_ref[:, lo] = pr
        hi_ref[:, lo] = pi

    ys = []
    for s in range(2):
        hr = xr_ref[:, s * half_lanes:(s + 1) * half_lanes].astype(BF16)
        hi = xi_ref[:, s * half_lanes:(s + 1) * half_lanes].astype(BF16)
        ys.append(jnp.dot(hr, cr_ref[s], preferred_element_type=F32)
                  + jnp.dot(hi, ci_ref[s], preferred_element_type=F32))
    y = jnp.concatenate(ys, axis=1) + d_ref[...] * u
    z = jnp.dot(jax.nn.gelu(y).astype(BF16), wglu_ref[...], preferred_element_type=F32)
    out = z[:, :SSM_WIDTH] * jax.nn.sigmoid(z[:, SSM_WIDTH:])
    for c in range(tiles):
        tb_ref[c] = out[:, c * LANES:(c + 1) * LANES]
    for b in range(batch):
        o_ref[b] = jnp.concatenate([tb_ref[c, pl.ds(b, steps, stride=batch), :] for c in range(tiles)],
                                   axis=1).astype(BF16)


def _ssm(u, wbr, wbi, a, cr, ci, d, wglu, batch, l):
    seq = u.shape[0] // batch
    rows = SSM_CHUNK * batch
    block = pl.BlockSpec((batch, SSM_CHUNK, SSM_WIDTH), lambda i: (0, i, 0))
    out = pl.pallas_call(
        _ssm_kernel,
        grid=(seq // SSM_CHUNK,),
        in_specs=[block] + [_layer(t, l) for t in (wbr, wbi, a, cr, ci, d, wglu)],
        out_specs=block,
        out_shape=jax.ShapeDtypeStruct((batch, seq, SSM_WIDTH), BF16),
        scratch_shapes=[pltpu.VMEM((SSM_WIDTH // LANES, rows, LANES), F32),
                        pltpu.VMEM((rows, SSM_LANES), F32), pltpu.VMEM((rows, SSM_LANES), F32),
                        pltpu.VMEM((SUBLANES, SSM_LANES // 2), F32), pltpu.VMEM((SUBLANES, SSM_LANES // 2), F32)],
        compiler_params=_params("arbitrary"),
        name="s5_ssm",
    )(u.reshape(batch, seq, SSM_WIDTH), wbr, wbi, a, cr, ci, d, wglu)
    return out.reshape(batch * seq, SSM_WIDTH)


def _merge_kernel(x_ref, oa_ref, gl_ref, om_ref, g_ref, wga_ref, wgs_ref, wgm_ref, woa_ref, wos_ref, wom_ref,
                  wout_ref, o_ref):
    x = x_ref[...]
    h = _rms(x, g_ref[...]).astype(BF16)
    mixed = None
    for b_ref, wg_ref, w_ref in ((oa_ref, wga_ref, woa_ref), (gl_ref, wgs_ref, wos_ref), (om_ref, wgm_ref, wom_ref)):
        gate = jax.nn.sigmoid(jnp.dot(h, wg_ref[...], preferred_element_type=F32))
        term = gate * jnp.dot(b_ref[...], w_ref[...], preferred_element_type=F32)
        mixed = term if mixed is None else mixed + term
    o_ref[...] = x + jnp.dot(mixed.astype(BF16), wout_ref[...], preferred_element_type=F32)


def _merge(x, oa, gl, om, gain, w_in, woa, wos, wom, wout, l):
    rows, d = x.shape
    tile = lambda n: pl.BlockSpec((ROW_TILE, n), lambda i: (i, 0))
    first = _PROJ_WIDTH // d
    gates = [pl.BlockSpec((None, d, d), lambda i, n=n: (l, 0, first + n)) for n in range(3)]
    return pl.pallas_call(
        _merge_kernel,
        grid=(rows // ROW_TILE,),
        in_specs=[tile(d), tile(oa.shape[1]), tile(gl.shape[1]), tile(om.shape[1]), _layer(gain, l)] + gates
                 + [_layer(t, l) for t in (woa, wos, wom, wout)],
        out_specs=tile(d),
        out_shape=jax.ShapeDtypeStruct((rows, d), F32),
        compiler_params=_params("parallel"),
        name="merge",
    )(x, oa, gl, om, gain, w_in, w_in, w_in, woa, wos, wom, wout)


def _ffn_kernel(x_ref, g_ref, w1_ref, w2_ref, gf_ref, o_ref, *, final_norm):
    x = x_ref[...]
    d = x.shape[1]
    h = _rms(x, g_ref[...]).astype(BF16)
    acc = x
    for c in range(w1_ref.shape[1] // d):
        a = jnp.maximum(jnp.dot(h, w1_ref[:, c * d:(c + 1) * d], preferred_element_type=F32), 0.0)
        acc = acc + jnp.dot((a * a).astype(BF16), w2_ref[c * d:(c + 1) * d, :], preferred_element_type=F32)
    o_ref[...] = _rms(acc, gf_ref[...]) if final_norm else acc


def _ffn(x, gain, w1, w2, gain_final, l, final_norm):
    rows, d = x.shape
    return pl.pallas_call(
        functools.partial(_ffn_kernel, final_norm=final_norm),
        grid=(rows // ROW_TILE,),
        in_specs=[pl.BlockSpec((ROW_TILE, d), lambda i: (i, 0)),
                  _layer(gain, l), _layer(w1, l), _layer(w2, l), _resident(gain_final)],
        out_specs=pl.BlockSpec((ROW_TILE, d), lambda i: (i, 0)),
        out_shape=jax.ShapeDtypeStruct((rows, d), F32),
        compiler_params=_params("parallel"),
        name="ffn",
    )(x, gain, w1, w2, gain_final)


def _alibi_slopes():
    n = 2 * N_HEADS
    return jnp.asarray(2.0 ** (-ALIBI_MAX_BIAS * np.arange(1, n + 1) / n), F32)


def _swa_constants(slopes):
    w = SWA_WINDOW
    s_rel = jnp.arange(2 * w, dtype=F32) - w
    e = jnp.zeros((2 * w, 2 * LANES), F32)
    for o in (HEAD_DIM, LANES):
        e = e.at[:, o:o + 3].set(s_rel[:, None]).at[:, o + 3:o + 6].set(1.0)
    t_rel = jnp.arange(w, dtype=F32)
    sl = slopes * LOG2E
    aug = jnp.zeros((N_HEADS, HEAD_DIM, w), F32)
    for c, part in enumerate(_split3(sl)):
        aug = aug.at[:, c, :].set(part[:, None])
    for c, part in enumerate(_split3(-sl[:, None] * t_rel[None, :])):
        aug = aug.at[:, 3 + c, :].set(part)
    return e.astype(BF16), aug


def _moba_constants(slopes, seq):
    assert seq // MOBA_BLOCK <= MOBA_COLS and (seq // MOBA_BLOCK) % VISIT == 0
    pos = jnp.arange(seq)
    blk = pos // MOBA_BLOCK
    off = (pos % MOBA_BLOCK).astype(F32)
    base = (blk * MOBA_BLOCK).astype(F32)
    n = MOBA_COLS
    onehot = (blk[:, None] == jnp.arange(n)[None, :]).astype(F32)
    e = jnp.zeros((seq, LANES), F32)
    for o in (MOBA_HI, 0):
        e = e.at[:, o:o + n].set(onehot)
        e = e.at[:, o + n:o + n + 3].set(off[:, None]).at[:, o + n + 3:o + n + 6].set(base[:, None])
    aug = jnp.zeros((N_HEADS, MOBA_COLS, LANES), F32)
    for c, part in enumerate(_split3(slopes * LOG2E)):
        aug = aug.at[:, c, :].set(part[:, None]).at[:, 3 + c, :].set(part[:, None])
    return e.astype(BF16), aug


def _ssm_weights(lam_re, lam_im, log_step, b_re, b_im, c_re, c_im):
    lr, li = lam_re.astype(F32), lam_im.astype(F32)
    dt = jnp.exp(log_step.astype(F32))[:, None]
    mag = jnp.exp(lr * dt)
    ab_re, ab_im = mag * jnp.cos(li * dt), mag * jnp.sin(li * dt)
    den = lr * lr + li * li
    nr, ni = ab_re - 1.0, ab_im
    f_re, f_im = (nr * lr + ni * li) / den, (ni * lr - nr * li) / den
    br, bi = b_re.astype(F32), b_im.astype(F32)
    bb_re = f_re[..., None] * br - f_im[..., None] * bi
    bb_im = f_re[..., None] * bi + f_im[..., None] * br
    half = SSM_GROUPS // 2
    eye = jnp.eye(half, dtype=F32)

    def in_map(bb):
        return jnp.stack([jnp.einsum('gph,gk->ghkp', bb[s * half:(s + 1) * half], eye)
                          .reshape(half * SSM_GROUP, half * SSM_STATE) for s in range(2)])

    def out_map(c):
        return jnp.stack([jnp.einsum('ghp,gk->kpgh', c[s * half:(s + 1) * half], eye)
                          .reshape(half * SSM_STATE, half * SSM_GROUP) for s in range(2)])

    a = jnp.stack([ab_re.reshape(-1), ab_im.reshape(-1)])
    return (in_map(bb_re).astype(BF16), in_map(bb_im).astype(BF16), a,
            out_map(c_re.astype(F32)).astype(BF16), out_map(-c_im.astype(F32)).astype(BF16))


def kernel(x, norm_mix, w_in, sinks, lam_re, lam_im, log_step, b_re, b_im, c_re, c_im, d_skip, w_glu,
           w_o_swa, w_o_ssm, w_o_moba, w_out, norm_ffn, w_ff1, w_ff2, norm_final):
    batch, seq, d = x.shape
    depth = w_in.shape[0]
    assert batch * 2 == SUBLANES, "the S5 scan packs two time steps of all batches into one vreg"
    slopes = _alibi_slopes()
    swa_e, swa_aug = _swa_constants(slopes[:N_HEADS])
    moba_e, moba_aug = _moba_constants(slopes[N_HEADS:], seq)

    w_all = w_in.astype(BF16)
    gain_mix = norm_mix.reshape(depth, 1, d)
    gain_ffn = norm_ffn.reshape(depth, 1, d)
    gain_final = norm_final.reshape(1, d)
    sink_tab = jnp.repeat(sinks.astype(F32) * LOG2E, SWA_WINDOW, axis=1).reshape(depth, 1, N_HEADS * SWA_WINDOW)
    wbr, wbi, a, cr, ci = jax.vmap(_ssm_weights)(lam_re, lam_im, log_step, b_re, b_im, c_re, c_im)
    skip = d_skip.astype(F32).reshape(depth, 1, SSM_WIDTH)
    wglu = w_glu.astype(BF16)
    woa, wos, wom = w_o_swa.astype(BF16), w_o_ssm.astype(BF16), w_o_moba.astype(BF16)
    wout, w1, w2 = w_out.astype(BF16), w_ff1.astype(BF16), w_ff2.astype(BF16)

    xs = x.reshape(batch * seq, d)
    for l in range(depth):
        qa, ka, va, u, qm, km, vm = _norm_proj(xs, gain_mix, w_all, l)
        oa = _swa(qa, ka, va, swa_e, swa_aug, sink_tab, batch, l)
        gl = _ssm(u, wbr, wbi, a, cr, ci, skip, wglu, batch, l)
        om = _moba(qm, km, vm, moba_e, moba_aug, batch)
        xs = _merge(xs, oa, gl, om, gain_mix, w_all, woa, wos, wom, wout, l)
        xs = _ffn(xs, gain_ffn, w1, w2, gain_final, l, final_norm=(l == depth - 1))
    return xs.reshape(batch, seq, d)
```

```python
import functools

import numpy as np
import jax
import jax.numpy as jnp
from jax import lax
from jax.experimental import pallas as pl
from jax.experimental.pallas import tpu as pltpu

F32 = jnp.float32
BF16 = jnp.bfloat16

HEAD_DIM = 64
N_HEADS = 8
KV_GROUP = 4
N_TILES = N_HEADS // 2
ATTN_SCALE = HEAD_DIM ** -0.5
LOG2E = 1.4426950408889634
SWA_WINDOW = 128
MOBA_BLOCK = 256
MOBA_TOPK = 3
SSM_GROUPS = 32
SSM_GROUP = 16
SSM_STATE = 64
SSM_WIDTH = SSM_GROUPS * SSM_GROUP
SSM_LANES = SSM_GROUPS * SSM_STATE
NORM_EPS = 1e-6
ALIBI_MAX_BIAS = 8.0
MASK_VALUE = -1e30
M_INIT = 0.5 * MASK_VALUE
GATE_FLOOR = -3.0e38

LANES = 128
SUBLANES = 8
BF16_ROWS = 16
VMEM_LIMIT = 56 * 1024 * 1024

ROW_TILE = 1024
SSM_CHUNK = 256
SWA_TILE = 1024
SCAN_LANES = 512
MOBA_COLS = 2 * SUBLANES
MOBA_HI = HEAD_DIM
VT_ROWS = HEAD_DIM + BF16_ROWS
VISIT = 2


def _params(*sem):
    return pltpu.CompilerParams(dimension_semantics=sem, vmem_limit_bytes=VMEM_LIMIT)


def _resident(arr):
    zeros = (0,) * arr.ndim
    return pl.BlockSpec(arr.shape, lambda *_: zeros)


def _layer(arr, l):
    zeros = (0,) * (arr.ndim - 1)
    return pl.BlockSpec((None,) + arr.shape[1:], lambda *_: (l,) + zeros)


def _rms(x, gain):
    return x * lax.rsqrt(jnp.mean(x * x, axis=-1, keepdims=True) + NORM_EPS) * gain


def _split3(x):
    hi = x.astype(BF16).astype(F32)
    mid = (x - hi).astype(BF16).astype(F32)
    lo = (x - hi - mid).astype(BF16).astype(F32)
    return hi, mid, lo


def _low_lanes():
    return lax.broadcasted_iota(jnp.int32, (1, LANES), 1) < HEAD_DIM


_PROJ_SIZES = (512, 128, 128, 512, 512, 128, 128)
_PROJ_WIDTH = sum(_PROJ_SIZES)
_Q_SCALE = ATTN_SCALE * LOG2E


def _norm_proj_kernel(x_ref, g_ref, w_ref, qa_ref, ka_ref, va_ref, u_ref, qm_ref, km_ref, vm_ref):
    h = _rms(x_ref[...], g_ref[...]).astype(BF16)
    proj = jnp.dot(h, w_ref[...], preferred_element_type=F32)
    start = 0
    for ref, size in zip((qa_ref, ka_ref, va_ref, u_ref, qm_ref, km_ref, vm_ref), _PROJ_SIZES):
        part = proj[:, start:start + size]
        if ref is qa_ref or ref is qm_ref:
            part = part * _Q_SCALE
        ref[...] = part.astype(ref.dtype)
        start += size


def _norm_proj(x, gain, w, l):
    rows, d = x.shape
    dts = (BF16, BF16, BF16, F32, BF16, BF16, BF16)
    return pl.pallas_call(
        _norm_proj_kernel,
        grid=(rows // ROW_TILE,),
        in_specs=[pl.BlockSpec((ROW_TILE, d), lambda i: (i, 0)), _layer(gain, l),
                  pl.BlockSpec((None, d, _PROJ_WIDTH), lambda i: (l, 0, 0))],
        out_specs=[pl.BlockSpec((ROW_TILE, n), lambda i: (i, 0)) for n in _PROJ_SIZES],
        out_shape=[jax.ShapeDtypeStruct((rows, n), dt) for n, dt in zip(_PROJ_SIZES, dts)],
        compiler_params=_params("parallel"),
        name="norm_proj",
    )(x, gain, w)


def _swa_kernel(q_ref, k_ref, v_ref, e_ref, aug_ref, sink_ref, o_ref, kc_ref, vt_ref, qt_ref, mb_ref):
    qi = pl.program_id(1)
    w = SWA_WINDOW
    half_cols = KV_GROUP * w
    low = _low_lanes()

    @pl.when(qi == 0)
    def _():
        k = k_ref[...]
        zero = jnp.zeros_like(k)
        kc_ref[:, 0:LANES] = jnp.where(low, k, zero)
        kc_ref[:, LANES:] = jnp.where(low, zero, k)
        vt = v_ref[...].astype(F32).T
        ones_row = jnp.where(lax.broadcasted_iota(jnp.int32, (BF16_ROWS, w), 0) == 0, 1.0, 0.0).astype(BF16)
        for n in range(vt_ref.shape[0]):
            for half in range(2):
                r0 = half * VT_ROWS
                vt_ref[n, r0:r0 + HEAD_DIM, :] = vt[half * HEAD_DIM:(half + 1) * HEAD_DIM,
                                                    n * w:(n + 1) * w].astype(BF16)
                vt_ref[n, r0 + HEAD_DIM:r0 + VT_ROWS, :] = ones_row
        qt_ref[...] = jnp.zeros(qt_ref.shape, BF16)
        key = lax.broadcasted_iota(jnp.int32, (2 * w, w), 0)
        qry = lax.broadcasted_iota(jnp.int32, (2 * w, w), 1)
        band = (key > qry) & (key <= qry + w)
        mb_ref[0] = jnp.where(band, 0.0, MASK_VALUE)
        mb_ref[1] = jnp.where(band & (key >= w), 0.0, MASK_VALUE)

    lane2 = lax.broadcasted_iota(jnp.int32, (1, 2 * LANES), 1)
    e_lanes = (lane2 >= HEAD_DIM) & (lane2 < 2 * LANES - HEAD_DIM)
    e_cols = e_ref[...]
    sink = sink_ref[...]
    for blk in range(SWA_TILE // w):
        t0 = qi * SWA_TILE + blk * w
        prev = jnp.maximum(t0 - w, 0)
        kwin = jnp.concatenate([kc_ref[pl.ds(pl.multiple_of(prev, w), w), :],
                                kc_ref[pl.ds(pl.multiple_of(t0, w), w), :]], axis=0)
        kcat = jnp.where(e_lanes, e_cols, kwin)
        vtw = jnp.concatenate([vt_ref[prev // w], vt_ref[t0 // w]], axis=1)
        for tile in range(N_TILES):
            qpt = q_ref[blk * w:(blk + 1) * w, tile * LANES:(tile + 1) * LANES].astype(F32).T
            for r in range(2):
                head = 2 * tile + r
                qh = qpt[r * HEAD_DIM:(r + 1) * HEAD_DIM]
                group = head // KV_GROUP
                rows = [qh, aug_ref[head]] if group == 0 else [aug_ref[head], qh]
                qt_ref[blk, group * LANES:(group + 1) * LANES, head * w:(head + 1) * w] = (
                    jnp.concatenate(rows, axis=0).astype(BF16))
        mb = mb_ref[(t0 == 0).astype(jnp.int32)]
        s = jnp.dot(kcat, qt_ref[blk], preferred_element_type=F32) + jnp.concatenate([mb] * N_HEADS, axis=1)
        m = jnp.maximum(jnp.max(s, axis=0, keepdims=True), sink)
        p = jnp.exp2((s - m).astype(BF16))
        pv = jnp.concatenate(
            [jnp.dot(vtw[half * VT_ROWS:(half + 1) * VT_ROWS], p[:, half * half_cols:(half + 1) * half_cols],
                     preferred_element_type=F32) for half in range(2)], axis=1)
        o_t = pv[0:HEAD_DIM] / (pv[HEAD_DIM:HEAD_DIM + 1] + jnp.exp2(sink - m))
        for tile in range(N_TILES):
            both = jnp.concatenate([o_t[:, (2 * tile + r) * w:(2 * tile + r + 1) * w] for r in range(2)], axis=0)
            o_ref[blk * w:(blk + 1) * w, tile * LANES:(tile + 1) * LANES] = both.T.astype(BF16)


def _swa(q, k, v, e_cols, aug, sinks, batch, l):
    seq = q.shape[0] // batch
    qw = N_HEADS * HEAD_DIM
    kw = k.shape[1]
    nq = seq // SWA_TILE
    nwin = SWA_TILE // SWA_WINDOW
    return pl.pallas_call(
        _swa_kernel,
        grid=(batch, nq),
        in_specs=[pl.BlockSpec((SWA_TILE, qw), lambda b, i: (b * nq + i, 0)),
                  pl.BlockSpec((seq, kw), lambda b, i: (b, 0)),
                  pl.BlockSpec((seq, kw), lambda b, i: (b, 0)),
                  _resident(e_cols), _resident(aug), _layer(sinks, l)],
        out_specs=pl.BlockSpec((SWA_TILE, qw), lambda b, i: (b * nq + i, 0)),
        out_shape=jax.ShapeDtypeStruct(q.shape, BF16),
        scratch_shapes=[pltpu.VMEM((seq, 2 * kw), BF16),
                        pltpu.VMEM((seq // SWA_WINDOW, 2 * VT_ROWS, SWA_WINDOW), BF16),
                        pltpu.VMEM((nwin, 2 * LANES, N_HEADS * SWA_WINDOW), BF16),
                        pltpu.VMEM((2, 2 * SWA_WINDOW, SWA_WINDOW), F32)],
        compiler_params=_params("parallel", "arbitrary"),
        name="swa_attention",
    )(q, k, v, e_cols, aug, sinks)


def _select_bias(gate, blk, j):
    g = jnp.where(blk < j, gate, GATE_FLOOR)
    bias = jnp.where(blk == j, 0.0, MASK_VALUE)
    blk_f = blk.astype(F32)
    for _ in range(MOBA_TOPK):
        mx = jnp.max(g, axis=0, keepdims=True)
        idx = jnp.min(jnp.where(g == mx, blk_f, float(MOBA_COLS)), axis=0, keepdims=True)
        hit = (blk_f == idx) & (mx > GATE_FLOOR)
        bias = jnp.where(hit, 0.0, bias)
        g = jnp.where(hit, GATE_FLOOR, g)
    return bias


def _moba_kernel(q_ref, k_ref, v_ref, e_ref, aug_ref, o_ref, kc_ref, vt_ref, km_ref, qt_ref, cb_ref, *, nblk):
    j = pl.program_id(1)
    L = MOBA_BLOCK
    cols = KV_GROUP * L
    low = _low_lanes()

    @pl.when(j == 0)
    def _():
        k = k_ref[...]
        e = e_ref[...]
        kc_ref[:, 0:LANES] = jnp.where(low, k, e)
        kc_ref[:, LANES:] = jnp.where(low, e, k)
        vt = v_ref[...].astype(F32).T
        ones_row = jnp.where(lax.broadcasted_iota(jnp.int32, (BF16_ROWS, VISIT * L), 0) == 0,
                             1.0, 0.0).astype(BF16)
        for n in range(nblk // VISIT):
            for half in range(2):
                r0 = half * VT_ROWS
                vt_ref[n, r0:r0 + HEAD_DIM, :] = vt[half * HEAD_DIM:(half + 1) * HEAD_DIM,
                                                    n * VISIT * L:(n + 1) * VISIT * L].astype(BF16)
                vt_ref[n, r0 + HEAD_DIM:r0 + VT_ROWS, :] = ones_row
        qt_ref[...] = jnp.zeros(qt_ref.shape, BF16)
        kmean = jnp.mean(k.astype(F32).reshape(nblk, L, LANES), axis=1)
        swapped = pltpu.roll(kmean, HEAD_DIM, axis=1)
        km_ref[...] = jnp.zeros(km_ref.shape, BF16)
        for group in range(2):
            placed = (jnp.where(low, kmean if group == 0 else swapped, 0.0),
                      jnp.where(low, 0.0, swapped if group == 0 else kmean))
            for r in range(2):
                for c, part in enumerate(_split3(placed[r])):
                    r0 = (3 * r + c) * BF16_ROWS
                    km_ref[group, r0:r0 + nblk, :] = part.astype(BF16)
        key = lax.broadcasted_iota(jnp.int32, (L, L), 0)
        qry = lax.broadcasted_iota(jnp.int32, (L, L), 1)
        cb_ref[0] = jnp.zeros((L, L), F32)
        cb_ref[1] = jnp.where(key <= qry, 0.0, MASK_VALUE)

    qpts, gates = [], []
    for tile in range(N_TILES):
        group = (2 * tile) // KV_GROUP
        qpt = q_ref[:, tile * LANES:(tile + 1) * LANES].astype(F32).T
        g = jnp.dot(km_ref[group], qpt.astype(BF16), preferred_element_type=F32)
        for r in range(2):
            r0 = 3 * BF16_ROWS * r
            gates.append(g[r0:r0 + BF16_ROWS] + g[r0 + BF16_ROWS:r0 + 2 * BF16_ROWS]
                         + g[r0 + 2 * BF16_ROWS:r0 + 3 * BF16_ROWS])
        qpts.append(qpt)
    blk = lax.broadcasted_iota(jnp.int32, (MOBA_COLS, N_HEADS * L), 0)
    bias_all = _select_bias(jnp.concatenate(gates, axis=1), blk, j)
    for head in range(N_HEADS):
        group = head // KV_GROUP
        extra = jnp.concatenate([bias_all[:, head * L:(head + 1) * L],
                                 jnp.concatenate([aug_ref[head]] * (L // LANES), axis=1),
                                 jnp.zeros((HEAD_DIM - 2 * MOBA_COLS, L), F32)], axis=0)
        qh = qpts[head // 2][(head % 2) * HEAD_DIM:(head % 2 + 1) * HEAD_DIM]
        rows = [qh, extra] if group == 0 else [extra, qh]
        qt_ref[group * LANES:(group + 1) * LANES, head * L:(head + 1) * L] = (
            jnp.concatenate(rows, axis=0).astype(BF16))

    def visit(n, count, carry):
        m, acc = carry
        cb = jnp.concatenate([cb_ref[(VISIT * n + i == j).astype(jnp.int32)] for i in range(count)], axis=0)
        cb = jnp.concatenate([cb] * N_HEADS, axis=1)
        off = pl.multiple_of(n * (VISIT * L), VISIT * L)
        s = jnp.dot(kc_ref[pl.ds(off, count * L), :], qt_ref[...], preferred_element_type=F32) + cb
        m_new = jnp.maximum(m, jnp.max(s, axis=0, keepdims=True))
        p = jnp.exp2((s - m_new).astype(BF16))
        pv = jnp.concatenate(
            [jnp.dot(vt_ref[n, half * VT_ROWS:(half + 1) * VT_ROWS, 0:count * L], p[:, half * cols:(half + 1) * cols],
                     preferred_element_type=F32) for half in range(2)], axis=1)
        return m_new, jnp.exp2(m - m_new) * acc + pv

    init = (jnp.full((1, 2 * cols), M_INIT, F32), jnp.zeros((VT_ROWS, 2 * cols), F32))
    full = (j + 1) // VISIT
    carry = lax.fori_loop(0, full, lambda n, c: visit(n, VISIT, c), init)
    _, acc = lax.cond((j + 1) % VISIT == 1, lambda c: visit(full, 1, c), lambda c: c, carry)
    for tile in range(N_TILES):
        parts = []
        for r in range(2):
            a = acc[:, (2 * tile + r) * L:(2 * tile + r + 1) * L]
            parts.append(a[0:HEAD_DIM] / a[HEAD_DIM:HEAD_DIM + 1])
        o_ref[:, tile * LANES:(tile + 1) * LANES] = jnp.concatenate(parts, axis=0).T.astype(BF16)


def _moba(q, k, v, e_cols, aug, batch):
    seq = q.shape[0] // batch
    nblk = seq // MOBA_BLOCK
    qw = N_HEADS * HEAD_DIM
    kw = k.shape[1]
    return pl.pallas_call(
        functools.partial(_moba_kernel, nblk=nblk),
        grid=(batch, nblk),
        in_specs=[pl.BlockSpec((MOBA_BLOCK, qw), lambda b, i: (b * nblk + i, 0)),
                  pl.BlockSpec((seq, kw), lambda b, i: (b, 0)),
                  pl.BlockSpec((seq, kw), lambda b, i: (b, 0)),
                  _resident(e_cols), _resident(aug)],
        out_specs=pl.BlockSpec((MOBA_BLOCK, qw), lambda b, i: (b * nblk + i, 0)),
        out_shape=jax.ShapeDtypeStruct(q.shape, BF16),
        scratch_shapes=[pltpu.VMEM((seq, 2 * kw), BF16),
                        pltpu.VMEM((nblk // VISIT, 2 * VT_ROWS, VISIT * MOBA_BLOCK), BF16),
                        pltpu.VMEM((2, 6 * BF16_ROWS, LANES), BF16),
                        pltpu.VMEM((2 * LANES, N_HEADS * MOBA_BLOCK), BF16),
                        pltpu.VMEM((2, MOBA_BLOCK, MOBA_BLOCK), F32)],
        compiler_params=_params("parallel", "arbitrary"),
        name="moba_attention",
    )(q, k, v, e_cols, aug)


def _ssm_kernel(u_ref, wbr_ref, wbi_ref, a_ref, cr_ref, ci_ref, d_ref, wglu_ref, o_ref,
                tb_ref, xr_ref, xi_ref, hr_ref, hi_ref):
    batch, steps, _ = u_ref.shape
    rows = batch * steps
    half_lanes = SSM_LANES // 2
    half_width = SSM_WIDTH // 2

    @pl.when(pl.program_id(0) == 0)
    def _():
        hr_ref[...] = jnp.zeros(hr_ref.shape, F32)
        hi_ref[...] = jnp.zeros(hi_ref.shape, F32)

    tiles = SSM_WIDTH // LANES
    for b in range(batch):
        ub_rows = u_ref[b]
        for c in range(tiles):
            tb_ref[c, pl.ds(b, steps, stride=batch), :] = ub_rows[:, c * LANES:(c + 1) * LANES]
    u = jnp.concatenate([tb_ref[c] for c in range(tiles)], axis=1)
    ub = u.astype(BF16)
    for s in range(2):
        us = ub[:, s * half_width:(s + 1) * half_width]
        xr_ref[:, s * half_lanes:(s + 1) * half_lanes] = jnp.dot(us, wbr_ref[s], preferred_element_type=F32)
        xi_ref[:, s * half_lanes:(s + 1) * half_lanes] = jnp.dot(us, wbi_ref[s], preferred_element_type=F32)

    top = lax.broadcasted_iota(jnp.int32, (SUBLANES, SCAN_LANES), 0) < batch
    for c in range(half_lanes // SCAN_LANES):
        lo = slice(c * SCAN_LANES, (c + 1) * SCAN_LANES)
        hi = slice(half_lanes + c * SCAN_LANES, half_lanes + (c + 1) * SCAN_LANES)
        ar = jnp.where(top, a_ref[0:1, lo], a_ref[0:1, hi])
        ai = jnp.where(top, a_ref[1:2, lo], a_ref[1:2, hi])

        def fold(x_lo, x_hi):
            return (jnp.where(top, x_lo, pltpu.roll(x_hi, batch, axis=0)),
                    jnp.where(top, pltpu.roll(x_lo, batch, axis=0), x_hi))

        def body(k, carry, lo=lo, hi=hi, ar=ar, ai=ai):
            pr, pi = carry
            rws = pl.ds(pl.multiple_of(k * SUBLANES, SUBLANES), SUBLANES)
            x1r, x2r = fold(xr_ref[rws, lo], xr_ref[rws, hi])
            x1i, x2i = fold(xi_ref[rws, lo], xi_ref[rws, hi])
            h1r = ar * pr - ai * pi + x1r
            h1i = ar * pi + ai * pr + x1i
            h2r = ar * h1r - ai * h1i + x2r
            h2i = ar * h1i + ai * h1r + x2i
            xr_ref[rws, lo], xr_ref[rws, hi] = fold(h1r, h2r)
            xi_ref[rws, lo], xi_ref[rws, hi] = fold(h1i, h2i)
            return h2r, h2i

        pr, pi = lax.fori_loop(0, rows // SUBLANES, body, (hr_ref[:, lo], hi_ref[:, lo]))
        hr_ref[:, lo] = pr
        hi_ref[:, lo] = pi

    ys = []
    for s in range(2):
        hr = xr_ref[:, s * half_lanes:(s + 1) * half_lanes].astype(BF16)
        hi = xi_ref[:, s * half_lanes:(s + 1) * half_lanes].astype(BF16)
        ys.append(jnp.dot(hr, cr_ref[s], preferred_element_type=F32)
                  + jnp.dot(hi, ci_ref[s], preferred_element_type=F32))
    y = jnp.concatenate(ys, axis=1) + d_ref[...] * u
    z = jnp.dot(jax.nn.gelu(y).astype(BF16), wglu_ref[...], preferred_element_type=F32)
    out = z[:, :SSM_WIDTH] * jax.nn.sigmoid(z[:, SSM_WIDTH:])
    for c in range(tiles):
        tb_ref[c] = out[:, c * LANES:(c + 1) * LANES]
    for b in range(batch):
        o_ref[b] = jnp.concatenate([tb_ref[c, pl.ds(b, steps, stride=batch), :] for c in range(tiles)],
                                   axis=1).astype(BF16)


def _ssm(u, wbr, wbi, a, cr, ci, d, wglu, batch, l):
    seq = u.shape[0] // batch
    rows = SSM_CHUNK * batch
    block = pl.BlockSpec((batch, SSM_CHUNK, SSM_WIDTH), lambda i: (0, i, 0))
    out = pl.pallas_call(
        _ssm_kernel,
        grid=(seq // SSM_CHUNK,),
        in_specs=[block] + [_layer(t, l) for t in (wbr, wbi, a, cr, ci, d, wglu)],
        out_specs=block,
        out_shape=jax.ShapeDtypeStruct((batch, seq, SSM_WIDTH), BF16),
        scratch_shapes=[pltpu.VMEM((SSM_WIDTH // LANES, rows, LANES), F32),
                        pltpu.VMEM((rows, SSM_LANES), F32), pltpu.VMEM((rows, SSM_LANES), F32),
                        pltpu.VMEM((SUBLANES, SSM_LANES // 2), F32), pltpu.VMEM((SUBLANES, SSM_LANES // 2), F32)],
        compiler_params=_params("arbitrary"),
        name="s5_ssm",
    )(u.reshape(batch, seq, SSM_WIDTH), wbr, wbi, a, cr, ci, d, wglu)
    return out.reshape(batch * seq, SSM_WIDTH)


def _merge_kernel(x_ref, oa_ref, gl_ref, om_ref, g_ref, wga_ref, wgs_ref, wgm_ref, woa_ref, wos_ref, wom_ref,
                  wout_ref, o_ref):
    x = x_ref[...]
    h = _rms(x, g_ref[...]).astype(BF16)
    mixed = None
    for b_ref, wg_ref, w_ref in ((oa_ref, wga_ref, woa_ref), (gl_ref, wgs_ref, wos_ref), (om_ref, wgm_ref, wom_ref)):
        gate = jax.nn.sigmoid(jnp.dot(h, wg_ref[...], preferred_element_type=F32))
        term = gate * jnp.dot(b_ref[...], w_ref[...], preferred_element_type=F32)
        mixed = term if mixed is None else mixed + term
    o_ref[...] = x + jnp.dot(mixed.astype(BF16), wout_ref[...], preferred_element_type=F32)


def _merge(x, oa, gl, om, gain, w_in, woa, wos, wom, wout, l):
    rows, d = x.shape
    tile = lambda n: pl.BlockSpec((ROW_TILE, n), lambda i: (i, 0))
    first = _PROJ_WIDTH // d
    gates = [pl.BlockSpec((None, d, d), lambda i, n=n: (l, 0, first + n)) for n in range(3)]
    return pl.pallas_call(
        _merge_kernel,
        grid=(rows // ROW_TILE,),
        in_specs=[tile(d), tile(oa.shape[1]), tile(gl.shape[1]), tile(om.shape[1]), _layer(gain, l)] + gates
                 + [_layer(t, l) for t in (woa, wos, wom, wout)],
        out_specs=tile(d),
        out_shape=jax.ShapeDtypeStruct((rows, d), F32),
        compiler_params=_params("parallel"),
        name="merge",
    )(x, oa, gl, om, gain, w_in, w_in, w_in, woa, wos, wom, wout)


def _ffn_kernel(x_ref, g_ref, w1_ref, w2_ref, gf_ref, o_ref, *, final_norm):
    x = x_ref[...]
    d = x.shape[1]
    h = _rms(x, g_ref[...]).astype(BF16)
    acc = x
    for c in range(w1_ref.shape[1] // d):
        a = jnp.maximum(jnp.dot(h, w1_ref[:, c * d:(c + 1) * d], preferred_element_type=F32), 0.0)
        acc = acc + jnp.dot((a * a).astype(BF16), w2_ref[c * d:(c + 1) * d, :], preferred_element_type=F32)
    o_ref[...] = _rms(acc, gf_ref[...]) if final_norm else acc


def _ffn(x, gain, w1, w2, gain_final, l, final_norm):
    rows, d = x.shape
    return pl.pallas_call(
        functools.partial(_ffn_kernel, final_norm=final_norm),
        grid=(rows // ROW_TILE,),
        in_specs=[pl.BlockSpec((ROW_TILE, d), lambda i: (i, 0)),
                  _layer(gain, l), _layer(w1, l), _layer(w2, l), _resident(gain_final)],
        out_specs=pl.BlockSpec((ROW_TILE, d), lambda i: (i, 0)),
        out_shape=jax.ShapeDtypeStruct((rows, d), F32),
        compiler_params=_params("parallel"),
        name="ffn",
    )(x, gain, w1, w2, gain_final)


def _alibi_slopes():
    n = 2 * N_HEADS
    return jnp.asarray(2.0 ** (-ALIBI_MAX_BIAS * np.arange(1, n + 1) / n), F32)


def _swa_constants(slopes):
    w = SWA_WINDOW
    s_rel = jnp.arange(2 * w, dtype=F32) - w
    e = jnp.zeros((2 * w, 2 * LANES), F32)
    for o in (HEAD_DIM, LANES):
        e = e.at[:, o:o + 3].set(s_rel[:, None]).at[:, o + 3:o + 6].set(1.0)
    t_rel = jnp.arange(w, dtype=F32)
    sl = slopes * LOG2E
    aug = jnp.zeros((N_HEADS, HEAD_DIM, w), F32)
    for c, part in enumerate(_split3(sl)):
        aug = aug.at[:, c, :].set(part[:, None])
    for c, part in enumerate(_split3(-sl[:, None] * t_rel[None, :])):
        aug = aug.at[:, 3 + c, :].set(part)
    return e.astype(BF16), aug


def _moba_constants(slopes, seq):
    assert seq // MOBA_BLOCK <= MOBA_COLS and (seq // MOBA_BLOCK) % VISIT == 0
    pos = jnp.arange(seq)
    blk = pos // MOBA_BLOCK
    off = (pos % MOBA_BLOCK).astype(F32)
    base = (blk * MOBA_BLOCK).astype(F32)
    n = MOBA_COLS
    onehot = (blk[:, None] == jnp.arange(n)[None, :]).astype(F32)
    e = jnp.zeros((seq, LANES), F32)
    for o in (MOBA_HI, 0):
        e = e.at[:, o:o + n].set(onehot)
        e = e.at[:, o + n:o + n + 3].set(off[:, None]).at[:, o + n + 3:o + n + 6].set(base[:, None])
    aug = jnp.zeros((N_HEADS, MOBA_COLS, LANES), F32)
    for c, part in enumerate(_split3(slopes * LOG2E)):
        aug = aug.at[:, c, :].set(part[:, None]).at[:, 3 + c, :].set(part[:, None])
    return e.astype(BF16), aug


def _ssm_weights(lam_re, lam_im, log_step, b_re, b_im, c_re, c_im):
    lr, li = lam_re.astype(F32), lam_im.astype(F32)
    dt = jnp.exp(log_step.astype(F32))[:, None]
    mag = jnp.exp(lr * dt)
    ab_re, ab_im = mag * jnp.cos(li * dt), mag * jnp.sin(li * dt)
    den = lr * lr + li * li
    nr, ni = ab_re - 1.0, ab_im
    f_re, f_im = (nr * lr + ni * li) / den, (ni * lr - nr * li) / den
    br, bi = b_re.astype(F32), b_im.astype(F32)
    bb_re = f_re[..., None] * br - f_im[..., None] * bi
    bb_im = f_re[..., None] * bi + f_im[..., None] * br
    half = SSM_GROUPS // 2
    row_group = jnp.arange(half * SSM_GROUP) // SSM_GROUP
    lane_group = jnp.arange(half * SSM_STATE) // SSM_STATE

    def in_map(bb):
        rows = jnp.swapaxes(bb, 1, 2).reshape(2, half * SSM_GROUP, SSM_STATE)
        return jnp.where(row_group[:, None] == lane_group[None, :], jnp.tile(rows, (1, 1, half)), 0.0)

    def out_map(c):
        rows = jnp.swapaxes(c, 1, 2).reshape(2, half * SSM_STATE, SSM_GROUP)
        return jnp.where(lane_group[:, None] == row_group[None, :], jnp.tile(rows, (1, 1, half)), 0.0)

    a = jnp.stack([ab_re.reshape(-1), ab_im.reshape(-1)])
    return (in_map(bb_re).astype(BF16), in_map(bb_im).astype(BF16), a,
            out_map(c_re.astype(F32)).astype(BF16), out_map(-c_im.astype(F32)).astype(BF16))


def kernel(x, norm_mix, w_in, sinks, lam_re, lam_im, log_step, b_re, b_im, c_re, c_im, d_skip, w_glu,
           w_o_swa, w_o_ssm, w_o_moba, w_out, norm_ffn, w_ff1, w_ff2, norm_final):
    batch, seq, d = x.shape
    depth = w_in.shape[0]
    assert batch * 2 == SUBLANES, "the S5 scan packs two time steps of all batches into one vreg"
    slopes = _alibi_slopes()
    swa_e, swa_aug = _swa_constants(slopes[:N_HEADS])
    moba_e, moba_aug = _moba_constants(slopes[N_HEADS:], seq)

    w_all = w_in.astype(BF16)
    gain_mix = norm_mix.reshape(depth, 1, d)
    gain_ffn = norm_ffn.reshape(depth, 1, d)
    gain_final = norm_final.reshape(1, d)
    sink_tab = jnp.repeat(sinks.astype(F32) * LOG2E, SWA_WINDOW, axis=1).reshape(depth, 1, N_HEADS * SWA_WINDOW)
    wbr, wbi, a, cr, ci = jax.vmap(_ssm_weights)(lam_re, lam_im, log_step, b_re, b_im, c_re, c_im)
    skip = d_skip.astype(F32).reshape(depth, 1, SSM_WIDTH)
    wglu = w_glu.astype(BF16)
    woa, wos, wom = w_o_swa.astype(BF16), w_o_ssm.astype(BF16), w_o_moba.astype(BF16)
    wout, w1, w2 = w_out.astype(BF16), w_ff1.astype(BF16), w_ff2.astype(BF16)

    xs = x.reshape(batch * seq, d)
    for l in range(depth):
        qa, ka, va, u, qm, km, vm = _norm_proj(xs, gain_mix, w_all, l)
        oa = _swa(qa, ka, va, swa_e, swa_aug, sink_tab, batch, l)
        gl = _ssm(u, wbr, wbi, a, cr, ci, skip, wglu, batch, l)
        om = _moba(qm, km, vm, moba_e, moba_aug, batch)
        xs = _merge(xs, oa, gl, om, gain_mix, w_all, woa, wos, wom, wout, l)
        xs = _ffn(xs, gain_ffn, w1, w2, gain_final, l, final_norm=(l == depth - 1))
    return xs.reshape(batch, seq, d)
```

```python
import functools

import numpy as np
import jax
import jax.numpy as jnp
from jax import lax
from jax.experimental import pallas as pl
from jax.experimental.pallas import tpu as pltpu

F32 = jnp.float32
BF16 = jnp.bfloat16

HEAD_DIM = 64
N_HEADS = 8
KV_GROUP = 4
N_TILES = N_HEADS // 2
ATTN_SCALE = HEAD_DIM ** -0.5
LOG2E = 1.4426950408889634
SWA_WINDOW = 128
MOBA_BLOCK = 256
MOBA_TOPK = 3
SSM_GROUPS = 32
SSM_GROUP = 16
SSM_STATE = 64
SSM_WIDTH = SSM_GROUPS * SSM_GROUP
SSM_LANES = SSM_GROUPS * SSM_STATE
NORM_EPS = 1e-6
ALIBI_MAX_BIAS = 8.0
MASK_VALUE = -1e30
M_INIT = 0.5 * MASK_VALUE
GATE_FLOOR = -3.0e38

LANES = 128
SUBLANES = 8
BF16_ROWS = 16
VMEM_LIMIT = 56 * 1024 * 1024

ROW_TILE = 1024
SSM_CHUNK = 256
SWA_TILE = 1024
SCAN_LANES = 512
MOBA_COLS = 2 * SUBLANES
MOBA_HI = HEAD_DIM
VT_ROWS = HEAD_DIM + BF16_ROWS
VISIT = 2


def _params(*sem):
    return pltpu.CompilerParams(dimension_semantics=sem, vmem_limit_bytes=VMEM_LIMIT)


def _resident(arr):
    zeros = (0,) * arr.ndim
    return pl.BlockSpec(arr.shape, lambda *_: zeros)


def _layer(arr, l):
    zeros = (0,) * (arr.ndim - 1)
    return pl.BlockSpec((None,) + arr.shape[1:], lambda *_: (l,) + zeros)


def _rms(x, gain):
    return x * lax.rsqrt(jnp.mean(x * x, axis=-1, keepdims=True) + NORM_EPS) * gain


def _split3(x):
    hi = x.astype(BF16).astype(F32)
    mid = (x - hi).astype(BF16).astype(F32)
    lo = (x - hi - mid).astype(BF16).astype(F32)
    return hi, mid, lo


def _low_lanes():
    return lax.broadcasted_iota(jnp.int32, (1, LANES), 1) < HEAD_DIM


_PROJ_SIZES = (512, 128, 128, 512, 512, 128, 128)
_PROJ_WIDTH = sum(_PROJ_SIZES)
_Q_SCALE = ATTN_SCALE * LOG2E


def _norm_proj_kernel(x_ref, g_ref, w_ref, qa_ref, ka_ref, va_ref, u_ref, qm_ref, km_ref, vm_ref):
    h = _rms(x_ref[...], g_ref[...]).astype(BF16)
    proj = jnp.dot(h, w_ref[...], preferred_element_type=F32)
    start = 0
    for ref, size in zip((qa_ref, ka_ref, va_ref, u_ref, qm_ref, km_ref, vm_ref), _PROJ_SIZES):
        part = proj[:, start:start + size]
        if ref is qa_ref or ref is qm_ref:
            part = part * _Q_SCALE
        ref[...] = part.astype(ref.dtype)
        start += size


def _norm_proj(x, gain, w, l):
    rows, d = x.shape
    dts = (BF16, BF16, BF16, F32, BF16, BF16, BF16)
    return pl.pallas_call(
        _norm_proj_kernel,
        grid=(rows // ROW_TILE,),
        in_specs=[pl.BlockSpec((ROW_TILE, d), lambda i: (i, 0)), _layer(gain, l),
                  pl.BlockSpec((None, d, _PROJ_WIDTH), lambda i: (l, 0, 0))],
        out_specs=[pl.BlockSpec((ROW_TILE, n), lambda i: (i, 0)) for n in _PROJ_SIZES],
        out_shape=[jax.ShapeDtypeStruct((rows, n), dt) for n, dt in zip(_PROJ_SIZES, dts)],
        compiler_params=_params("parallel"),
        name="norm_proj",
    )(x, gain, w)


def _swa_kernel(q_ref, k_ref, v_ref, e_ref, aug_ref, sink_ref, o_ref, kc_ref, vt_ref, qt_ref, mb_ref):
    qi = pl.program_id(1)
    w = SWA_WINDOW
    half_cols = KV_GROUP * w
    low = _low_lanes()

    @pl.when(qi == 0)
    def _():
        k = k_ref[...]
        zero = jnp.zeros_like(k)
        kc_ref[:, 0:LANES] = jnp.where(low, k, zero)
        kc_ref[:, LANES:] = jnp.where(low, zero, k)
        vt = v_ref[...].astype(F32).T
        ones_row = jnp.where(lax.broadcasted_iota(jnp.int32, (BF16_ROWS, w), 0) == 0, 1.0, 0.0).astype(BF16)
        for n in range(vt_ref.shape[0]):
            for half in range(2):
                r0 = half * VT_ROWS
                vt_ref[n, r0:r0 + HEAD_DIM, :] = vt[half * HEAD_DIM:(half + 1) * HEAD_DIM,
                                                    n * w:(n + 1) * w].astype(BF16)
                vt_ref[n, r0 + HEAD_DIM:r0 + VT_ROWS, :] = ones_row
        qt_ref[...] = jnp.zeros(qt_ref.shape, BF16)
        key = lax.broadcasted_iota(jnp.int32, (2 * w, w), 0)
        qry = lax.broadcasted_iota(jnp.int32, (2 * w, w), 1)
        band = (key > qry) & (key <= qry + w)
        mb_ref[0] = jnp.where(band, 0.0, MASK_VALUE)
        mb_ref[1] = jnp.where(band & (key >= w), 0.0, MASK_VALUE)

    lane2 = lax.broadcasted_iota(jnp.int32, (1, 2 * LANES), 1)
    e_lanes = (lane2 >= HEAD_DIM) & (lane2 < 2 * LANES - HEAD_DIM)
    e_cols = e_ref[...]
    sink = sink_ref[...]
    for blk in range(SWA_TILE // w):
        t0 = qi * SWA_TILE + blk * w
        prev = jnp.maximum(t0 - w, 0)
        kwin = jnp.concatenate([kc_ref[pl.ds(pl.multiple_of(prev, w), w), :],
                                kc_ref[pl.ds(pl.multiple_of(t0, w), w), :]], axis=0)
        kcat = jnp.where(e_lanes, e_cols, kwin)
        vtw = jnp.concatenate([vt_ref[prev // w], vt_ref[t0 // w]], axis=1)
        for tile in range(N_TILES):
            qpt = q_ref[blk * w:(blk + 1) * w, tile * LANES:(tile + 1) * LANES].astype(F32).T
            for r in range(2):
                head = 2 * tile + r
                qh = qpt[r * HEAD_DIM:(r + 1) * HEAD_DIM]
                group = head // KV_GROUP
                rows = [qh, aug_ref[head]] if group == 0 else [aug_ref[head], qh]
                qt_ref[blk, group * LANES:(group + 1) * LANES, head * w:(head + 1) * w] = (
                    jnp.concatenate(rows, axis=0).astype(BF16))
        mb = mb_ref[(t0 == 0).astype(jnp.int32)]
        s = jnp.dot(kcat, qt_ref[blk], preferred_element_type=F32) + jnp.concatenate([mb] * N_HEADS, axis=1)
        m = jnp.maximum(jnp.max(s, axis=0, keepdims=True), sink)
        p = jnp.exp2((s - m).astype(BF16))
        pv = jnp.concatenate(
            [jnp.dot(vtw[half * VT_ROWS:(half + 1) * VT_ROWS], p[:, half * half_cols:(half + 1) * half_cols],
                     preferred_element_type=F32) for half in range(2)], axis=1)
        o_t = pv[0:HEAD_DIM] / (pv[HEAD_DIM:HEAD_DIM + 1] + jnp.exp2(sink - m))
        for tile in range(N_TILES):
            both = jnp.concatenate([o_t[:, (2 * tile + r) * w:(2 * tile + r + 1) * w] for r in range(2)], axis=0)
            o_ref[blk * w:(blk + 1) * w, tile * LANES:(tile + 1) * LANES] = both.T.astype(BF16)


def _swa(q, k, v, e_cols, aug, sinks, batch, l):
    seq = q.shape[0] // batch
    qw = N_HEADS * HEAD_DIM
    kw = k.shape[1]
    nq = seq // SWA_TILE
    nwin = SWA_TILE // SWA_WINDOW
    return pl.pallas_call(
        _swa_kernel,
        grid=(batch, nq),
        in_specs=[pl.BlockSpec((SWA_TILE, qw), lambda b, i: (b * nq + i, 0)),
                  pl.BlockSpec((seq, kw), lambda b, i: (b, 0)),
                  pl.BlockSpec((seq, kw), lambda b, i: (b, 0)),
                  _resident(e_cols), _resident(aug), _layer(sinks, l)],
        out_specs=pl.BlockSpec((SWA_TILE, qw), lambda b, i: (b * nq + i, 0)),
        out_shape=jax.ShapeDtypeStruct(q.shape, BF16),
        scratch_shapes=[pltpu.VMEM((seq, 2 * kw), BF16),
                        pltpu.VMEM((seq // SWA_WINDOW, 2 * VT_ROWS, SWA_WINDOW), BF16),
                        pltpu.VMEM((nwin, 2 * LANES, N_HEADS * SWA_WINDOW), BF16),
                        pltpu.VMEM((2, 2 * SWA_WINDOW, SWA_WINDOW), F32)],
        compiler_params=_params("parallel", "arbitrary"),
        name="swa_attention",
    )(q, k, v, e_cols, aug, sinks)


def _select_bias(gate, blk, j):
    g = jnp.where(blk < j, gate, GATE_FLOOR)
    bias = jnp.where(blk == j, 0.0, MASK_VALUE)
    blk_f = blk.astype(F32)
    for _ in range(MOBA_TOPK):
        mx = jnp.max(g, axis=0, keepdims=True)
        idx = jnp.min(jnp.where(g == mx, blk_f, float(MOBA_COLS)), axis=0, keepdims=True)
        hit = (blk_f == idx) & (mx > GATE_FLOOR)
        bias = jnp.where(hit, 0.0, bias)
        g = jnp.where(hit, GATE_FLOOR, g)
    return bias


def _moba_kernel(q_ref, k_ref, v_ref, e_ref, aug_ref, o_ref, kc_ref, vt_ref, km_ref, qt_ref, cb_ref, *, nblk):
    j = pl.program_id(1)
    L = MOBA_BLOCK
    cols = KV_GROUP * L
    low = _low_lanes()

    @pl.when(j == 0)
    def _():
        k = k_ref[...]
        e = e_ref[...]
        kc_ref[:, 0:LANES] = jnp.where(low, k, e)
        kc_ref[:, LANES:] = jnp.where(low, e, k)
        vt = v_ref[...].astype(F32).T
        ones_row = jnp.where(lax.broadcasted_iota(jnp.int32, (BF16_ROWS, VISIT * L), 0) == 0,
                             1.0, 0.0).astype(BF16)
        for n in range(nblk // VISIT):
            for half in range(2):
                r0 = half * VT_ROWS
                vt_ref[n, r0:r0 + HEAD_DIM, :] = vt[half * HEAD_DIM:(half + 1) * HEAD_DIM,
                                                    n * VISIT * L:(n + 1) * VISIT * L].astype(BF16)
                vt_ref[n, r0 + HEAD_DIM:r0 + VT_ROWS, :] = ones_row
        qt_ref[...] = jnp.zeros(qt_ref.shape, BF16)
        kmean = jnp.mean(k.astype(F32).reshape(nblk, L, LANES), axis=1)
        swapped = pltpu.roll(kmean, HEAD_DIM, axis=1)
        km_ref[...] = jnp.zeros(km_ref.shape, BF16)
        for group in range(2):
            placed = (jnp.where(low, kmean if group == 0 else swapped, 0.0),
                      jnp.where(low, 0.0, swapped if group == 0 else kmean))
            for r in range(2):
                for c, part in enumerate(_split3(placed[r])):
                    r0 = (3 * r + c) * BF16_ROWS
                    km_ref[group, r0:r0 + nblk, :] = part.astype(BF16)
        key = lax.broadcasted_iota(jnp.int32, (L, L), 0)
        qry = lax.broadcasted_iota(jnp.int32, (L, L), 1)
        cb_ref[0] = jnp.zeros((L, L), F32)
        cb_ref[1] = jnp.where(key <= qry, 0.0, MASK_VALUE)

    qpts, gates = [], []
    for tile in range(N_TILES):
        group = (2 * tile) // KV_GROUP
        qpt = q_ref[:, tile * LANES:(tile + 1) * LANES].astype(F32).T
        g = jnp.dot(km_ref[group], qpt.astype(BF16), preferred_element_type=F32)
        for r in range(2):
            r0 = 3 * BF16_ROWS * r
            gates.append(g[r0:r0 + BF16_ROWS] + g[r0 + BF16_ROWS:r0 + 2 * BF16_ROWS]
                         + g[r0 + 2 * BF16_ROWS:r0 + 3 * BF16_ROWS])
        qpts.append(qpt)
    blk = lax.broadcasted_iota(jnp.int32, (MOBA_COLS, N_HEADS * L), 0)
    bias_all = _select_bias(jnp.concatenate(gates, axis=1), blk, j)
    for head in range(N_HEADS):
        group = head // KV_GROUP
        extra = jnp.concatenate([bias_all[:, head * L:(head + 1) * L],
                                 jnp.concatenate([aug_ref[head]] * (L // LANES), axis=1),
                                 jnp.zeros((HEAD_DIM - 2 * MOBA_COLS, L), F32)], axis=0)
        qh = qpts[head // 2][(head % 2) * HEAD_DIM:(head % 2 + 1) * HEAD_DIM]
        rows = [qh, extra] if group == 0 else [extra, qh]
        qt_ref[group * LANES:(group + 1) * LANES, head * L:(head + 1) * L] = (
            jnp.concatenate(rows, axis=0).astype(BF16))

    def visit(n, count, carry):
        m, acc = carry
        cb = jnp.concatenate([cb_ref[(VISIT * n + i == j).astype(jnp.int32)] for i in range(count)], axis=0)
        cb = jnp.concatenate([cb] * N_HEADS, axis=1)
        off = pl.multiple_of(n * (VISIT * L), VISIT * L)
        s = jnp.dot(kc_ref[pl.ds(off, count * L), :], qt_ref[...], preferred_element_type=F32) + cb
        m_new = jnp.maximum(m, jnp.max(s, axis=0, keepdims=True))
        p = jnp.exp2((s - m_new).astype(BF16))
        pv = jnp.concatenate(
            [jnp.dot(vt_ref[n, half * VT_ROWS:(half + 1) * VT_ROWS, 0:count * L], p[:, half * cols:(half + 1) * cols],
                     preferred_element_type=F32) for half in range(2)], axis=1)
        return m_new, jnp.exp2(m - m_new) * acc + pv

    init = (jnp.full((1, 2 * cols), M_INIT, F32), jnp.zeros((VT_ROWS, 2 * cols), F32))
    full = (j + 1) // VISIT
    carry = lax.fori_loop(0, full, lambda n, c: visit(n, VISIT, c), init)
    _, acc = lax.cond((j + 1) % VISIT == 1, lambda c: visit(full, 1, c), lambda c: c, carry)
    for tile in range(N_TILES):
        parts = []
        for r in range(2):
            a = acc[:, (2 * tile + r) * L:(2 * tile + r + 1) * L]
            parts.append(a[0:HEAD_DIM] / a[HEAD_DIM:HEAD_DIM + 1])
        o_ref[:, tile * LANES:(tile + 1) * LANES] = jnp.concatenate(parts, axis=0).T.astype(BF16)


def _moba(q, k, v, e_cols, aug, batch):
    seq = q.shape[0] // batch
    nblk = seq // MOBA_BLOCK
    qw = N_HEADS * HEAD_DIM
    kw = k.shape[1]
    return pl.pallas_call(
        functools.partial(_moba_kernel, nblk=nblk),
        grid=(batch, nblk),
        in_specs=[pl.BlockSpec((MOBA_BLOCK, qw), lambda b, i: (b * nblk + i, 0)),
                  pl.BlockSpec((seq, kw), lambda b, i: (b, 0)),
                  pl.BlockSpec((seq, kw), lambda b, i: (b, 0)),
                  _resident(e_cols), _resident(aug)],
        out_specs=pl.BlockSpec((MOBA_BLOCK, qw), lambda b, i: (b * nblk + i, 0)),
        out_shape=jax.ShapeDtypeStruct(q.shape, BF16),
        scratch_shapes=[pltpu.VMEM((seq, 2 * kw), BF16),
                        pltpu.VMEM((nblk // VISIT, 2 * VT_ROWS, VISIT * MOBA_BLOCK), BF16),
                        pltpu.VMEM((2, 6 * BF16_ROWS, LANES), BF16),
                        pltpu.VMEM((2 * LANES, N_HEADS * MOBA_BLOCK), BF16),
                        pltpu.VMEM((2, MOBA_BLOCK, MOBA_BLOCK), F32)],
        compiler_params=_params("parallel", "arbitrary"),
        name="moba_attention",
    )(q, k, v, e_cols, aug)


def _ssm_kernel(u_ref, wbr_ref, wbi_ref, a_ref, cr_ref, ci_ref, d_ref, wglu_ref, o_ref,
                tb_ref, xr_ref, xi_ref, hr_ref, hi_ref):
    batch, steps, _ = u_ref.shape
    rows = batch * steps
    half_lanes = SSM_LANES // 2
    half_width = SSM_WIDTH // 2

    @pl.when(pl.program_id(0) == 0)
    def _():
        hr_ref[...] = jnp.zeros(hr_ref.shape, F32)
        hi_ref[...] = jnp.zeros(hi_ref.shape, F32)

    tiles = SSM_WIDTH // LANES
    for b in range(batch):
        ub_rows = u_ref[b]
        for c in range(tiles):
            tb_ref[c, pl.ds(b, steps, stride=batch), :] = ub_rows[:, c * LANES:(c + 1) * LANES]
    u = jnp.concatenate([tb_ref[c] for c in range(tiles)], axis=1)
    ub = u.astype(BF16)
    for s in range(2):
        us = ub[:, s * half_width:(s + 1) * half_width]
        xr_ref[:, s * half_lanes:(s + 1) * half_lanes] = jnp.dot(us, wbr_ref[s], preferred_element_type=F32)
        xi_ref[:, s * half_lanes:(s + 1) * half_lanes] = jnp.dot(us, wbi_ref[s], preferred_element_type=F32)

    top = lax.broadcasted_iota(jnp.int32, (SUBLANES, SCAN_LANES), 0) < batch
    for c in range(half_lanes // SCAN_LANES):
        lo = slice(c * SCAN_LANES, (c + 1) * SCAN_LANES)
        hi = slice(half_lanes + c * SCAN_LANES, half_lanes + (c + 1) * SCAN_LANES)
        ar = jnp.where(top, a_ref[0:1, lo], a_ref[0:1, hi])
        ai = jnp.where(top, a_ref[1:2, lo], a_ref[1:2, hi])

        def fold(x_lo, x_hi):
            return (jnp.where(top, x_lo, pltpu.roll(x_hi, batch, axis=0)),
                    jnp.where(top, pltpu.roll(x_lo, batch, axis=0), x_hi))

        def body(k, carry, lo=lo, hi=hi, ar=ar, ai=ai):
            pr, pi = carry
            rws = pl.ds(pl.multiple_of(k * SUBLANES, SUBLANES), SUBLANES)
            x1r, x2r = fold(xr_ref[rws, lo], xr_ref[rws, hi])
            x1i, x2i = fold(xi_ref[rws, lo], xi_ref[rws, hi])
            h1r = ar * pr - ai * pi + x1r
            h1i = ar * pi + ai * pr + x1i
            h2r = ar * h1r - ai * h1i + x2r
            h2i = ar * h1i + ai * h1r + x2i
            xr_ref[rws, lo], xr_ref[rws, hi] = fold(h1r, h2r)
            xi_ref[rws, lo], xi_ref[rws, hi] = fold(h1i, h2i)
            return h2r, h2i

        pr, pi = lax.fori_loop(0, rows // SUBLANES, body, (hr_ref[:, lo], hi_ref[:, lo]))
        hr_ref[:, lo] = pr
        hi_ref[:, lo] = pi

    ys = []
    for s in range(2):
        hr = xr_ref[:, s * half_lanes:(s + 1) * half_lanes].astype(BF16)
        hi = xi_ref[:, s * half_lanes:(s + 1) * half_lanes].astype(BF16)
        ys.append(jnp.dot(hr, cr_ref[s], preferred_element_type=F32)
                  + jnp.dot(hi, ci_ref[s], preferred_element_type=F32))
    y = jnp.concatenate(ys, axis=1) + d_ref[...] * u
    z = jnp.dot(jax.nn.gelu(y).astype(BF16), wglu_ref[...], preferred_element_type=F32)
    out = z[:, :SSM_WIDTH] * jax.nn.sigmoid(z[:, SSM_WIDTH:])
    for c in range(tiles):
        tb_ref[c] = out[:, c * LANES:(c + 1) * LANES]
    for b in range(batch):
        o_ref[b] = jnp.concatenate([tb_ref[c, pl.ds(b, steps, stride=batch), :] for c in range(tiles)],
                                   axis=1).astype(BF16)


def _ssm(u, wbr, wbi, a, cr, ci, d, wglu, batch, l):
    seq = u.shape[0] // batch
    rows = SSM_CHUNK * batch
    block = pl.BlockSpec((batch, SSM_CHUNK, SSM_WIDTH), lambda i: (0, i, 0))
    out = pl.pallas_call(
        _ssm_kernel,
        grid=(seq // SSM_CHUNK,),
        in_specs=[block] + [_layer(t, l) for t in (wbr, wbi, a, cr, ci, d, wglu)],
        out_specs=block,
        out_shape=jax.ShapeDtypeStruct((batch, seq, SSM_WIDTH), BF16),
        scratch_shapes=[pltpu.VMEM((SSM_WIDTH // LANES, rows, LANES), F32),
                        pltpu.VMEM((rows, SSM_LANES), F32), pltpu.VMEM((rows, SSM_LANES), F32),
                        pltpu.VMEM((SUBLANES, SSM_LANES // 2), F32), pltpu.VMEM((SUBLANES, SSM_LANES // 2), F32)],
        compiler_params=_params("arbitrary"),
        name="s5_ssm",
    )(u.reshape(batch, seq, SSM_WIDTH), wbr, wbi, a, cr, ci, d, wglu)
    return out.reshape(batch * seq, SSM_WIDTH)


def _merge_kernel(x_ref, oa_ref, gl_ref, om_ref, g_ref, wga_ref, wgs_ref, wgm_ref, woa_ref, wos_ref, wom_ref,
                  wout_ref, o_ref):
    x = x_ref[...]
    h = _rms(x, g_ref[...]).astype(BF16)
    mixed = None
    for b_ref, wg_ref, w_ref in ((oa_ref, wga_ref, woa_ref), (gl_ref, wgs_ref, wos_ref), (om_ref, wgm_ref, wom_ref)):
        gate = jax.nn.sigmoid(jnp.dot(h, wg_ref[...], preferred_element_type=F32))
        term = gate * jnp.dot(b_ref[...], w_ref[...], preferred_element_type=F32)
        mixed = term if mixed is None else mixed + term
    o_ref[...] = x + jnp.dot(mixed.astype(BF16), wout_ref[...], preferred_element_type=F32)


def _merge(x, oa, gl, om, gain, w_in, woa, wos, wom, wout, l):
    rows, d = x.shape
    tile = lambda n: pl.BlockSpec((ROW_TILE, n), lambda i: (i, 0))
    first = _PROJ_WIDTH // d
    gates = [pl.BlockSpec((None, d, d), lambda i, n=n: (l, 0, first + n)) for n in range(3)]
    return pl.pallas_call(
        _merge_kernel,
        grid=(rows // ROW_TILE,),
        in_specs=[tile(d), tile(oa.shape[1]), tile(gl.shape[1]), tile(om.shape[1]), _layer(gain, l)] + gates
                 + [_layer(t, l) for t in (woa, wos, wom, wout)],
        out_specs=tile(d),
        out_shape=jax.ShapeDtypeStruct((rows, d), F32),
        compiler_params=_params("parallel"),
        name="merge",
    )(x, oa, gl, om, gain, w_in, w_in, w_in, woa, wos, wom, wout)


def _ffn_kernel(x_ref, g_ref, w1_ref, w2_ref, gf_ref, o_ref, *, final_norm):
    x = x_ref[...]
    d = x.shape[1]
    h = _rms(x, g_ref[...]).astype(BF16)
    acc = x
    for c in range(w1_ref.shape[1] // d):
        a = jnp.maximum(jnp.dot(h, w1_ref[:, c * d:(c + 1) * d], preferred_element_type=F32), 0.0)
        acc = acc + jnp.dot((a * a).astype(BF16), w2_ref[c * d:(c + 1) * d, :], preferred_element_type=F32)
    o_ref[...] = _rms(acc, gf_ref[...]) if final_norm else acc


def _ffn(x, gain, w1, w2, gain_final, l, final_norm):
    rows, d = x.shape
    return pl.pallas_call(
        functools.partial(_ffn_kernel, final_norm=final_norm),
        grid=(rows // ROW_TILE,),
        in_specs=[pl.BlockSpec((ROW_TILE, d), lambda i: (i, 0)),
                  _layer(gain, l), _layer(w1, l), _layer(w2, l), _resident(gain_final)],
        out_specs=pl.BlockSpec((ROW_TILE, d), lambda i: (i, 0)),
        out_shape=jax.ShapeDtypeStruct((rows, d), F32),
        compiler_params=_params("parallel"),
        name="ffn",
    )(x, gain, w1, w2, gain_final)


def _alibi_slopes():
    n = 2 * N_HEADS
    return np.asarray(2.0 ** (-ALIBI_MAX_BIAS * np.arange(1, n + 1) / n), np.float32)


def _swa_constants(slopes):
    w = SWA_WINDOW
    s_rel = np.arange(2 * w, dtype=np.float32) - w
    e = np.zeros((2 * w, 2 * LANES), np.float32)
    for o in (HEAD_DIM, LANES):
        e[:, o:o + 3] = s_rel[:, None]
        e[:, o + 3:o + 6] = 1.0
    t_rel = np.arange(w, dtype=np.float32)
    sl = slopes * np.float32(LOG2E)
    aug = np.zeros((N_HEADS, HEAD_DIM, w), np.float32)
    for c, part in enumerate(_split3(sl)):
        aug[:, c, :] = part[:, None]
    for c, part in enumerate(_split3(-sl[:, None] * t_rel[None, :])):
        aug[:, 3 + c, :] = part
    return e.astype(BF16), aug


def _moba_constants(slopes, seq):
    assert seq // MOBA_BLOCK <= MOBA_COLS and (seq // MOBA_BLOCK) % VISIT == 0
    pos = np.arange(seq)
    blk = pos // MOBA_BLOCK
    off = (pos % MOBA_BLOCK).astype(np.float32)
    base = (blk * MOBA_BLOCK).astype(np.float32)
    n = MOBA_COLS
    onehot = (blk[:, None] == np.arange(n)[None, :]).astype(np.float32)
    e = np.zeros((seq, LANES), np.float32)
    for o in (MOBA_HI, 0):
        e[:, o:o + n] = onehot
        e[:, o + n:o + n + 3] = off[:, None]
        e[:, o + n + 3:o + n + 6] = base[:, None]
    aug = np.zeros((N_HEADS, MOBA_COLS, LANES), np.float32)
    for c, part in enumerate(_split3(slopes * np.float32(LOG2E))):
        aug[:, c, :] = part[:, None]
        aug[:, 3 + c, :] = part[:, None]
    return e.astype(BF16), aug


def _ssm_weights(lam_re, lam_im, log_step, b_re, b_im, c_re, c_im):
    lr, li = lam_re.astype(F32), lam_im.astype(F32)
    dt = jnp.exp(log_step.astype(F32))[:, None]
    mag = jnp.exp(lr * dt)
    ab_re, ab_im = mag * jnp.cos(li * dt), mag * jnp.sin(li * dt)
    den = lr * lr + li * li
    nr, ni = ab_re - 1.0, ab_im
    f_re, f_im = (nr * lr + ni * li) / den, (ni * lr - nr * li) / den
    br, bi = b_re.astype(F32), b_im.astype(F32)
    bb_re = f_re[..., None] * br - f_im[..., None] * bi
    bb_im = f_re[..., None] * bi + f_im[..., None] * br
    half = SSM_GROUPS // 2
    row_group = jnp.arange(half * SSM_GROUP) // SSM_GROUP
    lane_group = jnp.arange(half * SSM_STATE) // SSM_STATE

    def in_map(bb):
        rows = jnp.swapaxes(bb, 1, 2).reshape(2, half * SSM_GROUP, SSM_STATE)
        return jnp.where(row_group[:, None] == lane_group[None, :], jnp.tile(rows, (1, 1, half)), 0.0)

    def out_map(c):
        rows = jnp.swapaxes(c, 1, 2).reshape(2, half * SSM_STATE, SSM_GROUP)
        return jnp.where(lane_group[:, None] == row_group[None, :], jnp.tile(rows, (1, 1, half)), 0.0)

    a = jnp.stack([ab_re.reshape(-1), ab_im.reshape(-1)])
    return (in_map(bb_re).astype(BF16), in_map(bb_im).astype(BF16), a,
            out_map(c_re.astype(F32)).astype(BF16), out_map(-c_im.astype(F32)).astype(BF16))


def kernel(x, norm_mix, w_in, sinks, lam_re, lam_im, log_step, b_re, b_im, c_re, c_im, d_skip, w_glu,
           w_o_swa, w_o_ssm, w_o_moba, w_out, norm_ffn, w_ff1, w_ff2, norm_final):
    batch, seq, d = x.shape
    depth = w_in.shape[0]
    assert batch * 2 == SUBLANES, "the S5 scan packs two time steps of all batches into one vreg"
    slopes = _alibi_slopes()
    swa_e, swa_aug = _swa_constants(slopes[:N_HEADS])
    moba_e, moba_aug = _moba_constants(slopes[N_HEADS:], seq)

    w_all = w_in.astype(BF16)
    gain_mix = norm_mix.reshape(depth, 1, d)
    gain_ffn = norm_ffn.reshape(depth, 1, d)
    gain_final = norm_final.reshape(1, d)
    sink_tab = jnp.repeat(sinks.astype(F32) * LOG2E, SWA_WINDOW, axis=1).reshape(depth, 1, N_HEADS * SWA_WINDOW)
    wbr, wbi, a, cr, ci = jax.vmap(_ssm_weights)(lam_re, lam_im, log_step, b_re, b_im, c_re, c_im)
    skip = d_skip.astype(F32).reshape(depth, 1, SSM_WIDTH)
    wglu = w_glu.astype(BF16)
    woa, wos, wom = w_o_swa.astype(BF16), w_o_ssm.astype(BF16), w_o_moba.astype(BF16)
    wout, w1, w2 = w_out.astype(BF16), w_ff1.astype(BF16), w_ff2.astype(BF16)

    xs = x.reshape(batch * seq, d)
    for l in range(depth):
        qa, ka, va, u, qm, km, vm = _norm_proj(xs, gain_mix, w_all, l)
        oa = _swa(qa, ka, va, swa_e, swa_aug, sink_tab, batch, l)
        gl = _ssm(u, wbr, wbi, a, cr, ci, skip, wglu, batch, l)
        om = _moba(qm, km, vm, moba_e, moba_aug, batch)
        xs = _merge(xs, oa, gl, om, gain_mix, w_all, woa, wos, wom, wout, l)
        xs = _ffn(xs, gain_ffn, w1, w2, gain_final, l, final_norm=(l == depth - 1))
    return xs.reshape(batch, seq, d)
```

```python
import functools

import numpy as np
import jax
import jax.numpy as jnp
from jax import lax
from jax.experimental import pallas as pl
from jax.experimental.pallas import tpu as pltpu

F32 = jnp.float32
BF16 = jnp.bfloat16

HEAD_DIM = 64
N_HEADS = 8
KV_GROUP = 4
N_TILES = N_HEADS // 2
ATTN_SCALE = HEAD_DIM ** -0.5
LOG2E = 1.4426950408889634
SWA_WINDOW = 128
MOBA_BLOCK = 256
MOBA_TOPK = 3
SSM_GROUPS = 32
SSM_GROUP = 16
SSM_STATE = 64
SSM_WIDTH = SSM_GROUPS * SSM_GROUP
SSM_LANES = SSM_GROUPS * SSM_STATE
NORM_EPS = 1e-6
ALIBI_MAX_BIAS = 8.0
MASK_VALUE = -1e30
M_INIT = 0.5 * MASK_VALUE
GATE_FLOOR = -3.0e38

LANES = 128
SUBLANES = 8
BF16_ROWS = 16
VMEM_LIMIT = 56 * 1024 * 1024

ROW_TILE = 1024
SSM_CHUNK = 256
SWA_TILE = 1024
SWA_LOOKAHEAD = 3
SCAN_LANES = 512
MOBA_COLS = 2 * SUBLANES
MOBA_HI = HEAD_DIM
VT_ROWS = HEAD_DIM + BF16_ROWS
VISIT = 2


def _params(*sem):
    return pltpu.CompilerParams(dimension_semantics=sem, vmem_limit_bytes=VMEM_LIMIT)


def _resident(arr):
    zeros = (0,) * arr.ndim
    return pl.BlockSpec(arr.shape, lambda *_: zeros)


def _layer(arr, l):
    zeros = (0,) * (arr.ndim - 1)
    return pl.BlockSpec((None,) + arr.shape[1:], lambda *_: (l,) + zeros)


def _rms(x, gain):
    return x * lax.rsqrt(jnp.mean(x * x, axis=-1, keepdims=True) + NORM_EPS) * gain


def _split3(x):
    hi = x.astype(BF16).astype(F32)
    mid = (x - hi).astype(BF16).astype(F32)
    lo = (x - hi - mid).astype(BF16).astype(F32)
    return hi, mid, lo


def _low_lanes():
    return lax.broadcasted_iota(jnp.int32, (1, LANES), 1) < HEAD_DIM


_PROJ_SIZES = (512, 128, 128, 512, 512, 128, 128)
_PROJ_WIDTH = sum(_PROJ_SIZES)
_Q_SCALE = ATTN_SCALE * LOG2E


def _norm_proj_kernel(x_ref, g_ref, w_ref, qa_ref, ka_ref, va_ref, u_ref, qm_ref, km_ref, vm_ref):
    h = _rms(x_ref[...], g_ref[...]).astype(BF16)
    proj = jnp.dot(h, w_ref[...], preferred_element_type=F32)
    start = 0
    for ref, size in zip((qa_ref, ka_ref, va_ref, u_ref, qm_ref, km_ref, vm_ref), _PROJ_SIZES):
        part = proj[:, start:start + size]
        if ref is qa_ref or ref is qm_ref:
            part = part * _Q_SCALE
        ref[...] = part.astype(ref.dtype)
        start += size


def _norm_proj(x, gain, w, l):
    rows, d = x.shape
    dts = (BF16, BF16, BF16, F32, BF16, BF16, BF16)
    return pl.pallas_call(
        _norm_proj_kernel,
        grid=(rows // ROW_TILE,),
        in_specs=[pl.BlockSpec((ROW_TILE, d), lambda i: (i, 0)), _layer(gain, l),
                  pl.BlockSpec((None, d, _PROJ_WIDTH), lambda i: (l, 0, 0))],
        out_specs=[pl.BlockSpec((ROW_TILE, n), lambda i: (i, 0)) for n in _PROJ_SIZES],
        out_shape=[jax.ShapeDtypeStruct((rows, n), dt) for n, dt in zip(_PROJ_SIZES, dts)],
        compiler_params=_params("parallel"),
        name="norm_proj",
    )(x, gain, w)


def _swa_kernel(q_ref, k_ref, v_ref, e_ref, aug_ref, sink_ref, o_ref, kc_ref, vt_ref, qt_ref, mb_ref):
    qi = pl.program_id(1)
    w = SWA_WINDOW
    half_cols = KV_GROUP * w
    low = _low_lanes()

    @pl.when(qi == 0)
    def _():
        k = k_ref[...]
        zero = jnp.zeros_like(k)
        kc_ref[:, 0:LANES] = jnp.where(low, k, zero)
        kc_ref[:, LANES:] = jnp.where(low, zero, k)
        vt = v_ref[...].astype(F32).T
        ones_row = jnp.where(lax.broadcasted_iota(jnp.int32, (BF16_ROWS, w), 0) == 0, 1.0, 0.0).astype(BF16)
        for n in range(vt_ref.shape[0]):
            for half in range(2):
                r0 = half * VT_ROWS
                vt_ref[n, r0:r0 + HEAD_DIM, :] = vt[half * HEAD_DIM:(half + 1) * HEAD_DIM,
                                                    n * w:(n + 1) * w].astype(BF16)
                vt_ref[n, r0 + HEAD_DIM:r0 + VT_ROWS, :] = ones_row
        qt_ref[...] = jnp.zeros(qt_ref.shape, BF16)
        key = lax.broadcasted_iota(jnp.int32, (2 * w, w), 0)
        qry = lax.broadcasted_iota(jnp.int32, (2 * w, w), 1)
        band = (key > qry) & (key <= qry + w)
        mb_ref[0] = jnp.where(band, 0.0, MASK_VALUE)
        mb_ref[1] = jnp.where(band & (key >= w), 0.0, MASK_VALUE)

    lane2 = lax.broadcasted_iota(jnp.int32, (1, 2 * LANES), 1)
    e_lanes = (lane2 >= HEAD_DIM) & (lane2 < 2 * LANES - HEAD_DIM)
    e_cols = e_ref[...]
    sink = sink_ref[...]
    def scores(blk):
        t0 = qi * SWA_TILE + blk * w
        prev = jnp.maximum(t0 - w, 0)
        kwin = jnp.concatenate([kc_ref[pl.ds(pl.multiple_of(prev, w), w), :],
                                kc_ref[pl.ds(pl.multiple_of(t0, w), w), :]], axis=0)
        kcat = jnp.where(e_lanes, e_cols, kwin)
        vtw = jnp.concatenate([vt_ref[prev // w], vt_ref[t0 // w]], axis=1)
        for tile in range(N_TILES):
            qpt = q_ref[blk * w:(blk + 1) * w, tile * LANES:(tile + 1) * LANES].astype(F32).T
            for r in range(2):
                head = 2 * tile + r
                qh = qpt[r * HEAD_DIM:(r + 1) * HEAD_DIM]
                group = head // KV_GROUP
                rows = [qh, aug_ref[head]] if group == 0 else [aug_ref[head], qh]
                qt_ref[blk, group * LANES:(group + 1) * LANES, head * w:(head + 1) * w] = (
                    jnp.concatenate(rows, axis=0).astype(BF16))
        mb = mb_ref[(t0 == 0).astype(jnp.int32)]
        s = jnp.dot(kcat, qt_ref[blk], preferred_element_type=F32) + jnp.concatenate([mb] * N_HEADS, axis=1)
        return s, jnp.max(s, axis=0, keepdims=True), vtw

    nwin = SWA_TILE // w
    ahead = [scores(b) for b in range(SWA_LOOKAHEAD)]
    for blk in range(nwin):
        s, smax, vtw = ahead.pop(0)
        m = jnp.maximum(smax, sink)
        if blk + SWA_LOOKAHEAD < nwin:
            ahead.append(scores(blk + SWA_LOOKAHEAD))
            m = jnp.maximum(m, jnp.minimum(m, ahead[-1][1]))
        p = jnp.exp2((s - m).astype(BF16))
        pv = jnp.concatenate(
            [jnp.dot(vtw[half * VT_ROWS:(half + 1) * VT_ROWS], p[:, half * half_cols:(half + 1) * half_cols],
                     preferred_element_type=F32) for half in range(2)], axis=1)
        o_t = pv[0:HEAD_DIM] / (pv[HEAD_DIM:HEAD_DIM + 1] + jnp.exp2(sink - m))
        for tile in range(N_TILES):
            both = jnp.concatenate([o_t[:, (2 * tile + r) * w:(2 * tile + r + 1) * w] for r in range(2)], axis=0)
            o_ref[blk * w:(blk + 1) * w, tile * LANES:(tile + 1) * LANES] = both.T.astype(BF16)


def _swa(q, k, v, e_cols, aug, sinks, batch, l):
    seq = q.shape[0] // batch
    qw = N_HEADS * HEAD_DIM
    kw = k.shape[1]
    nq = seq // SWA_TILE
    nwin = SWA_TILE // SWA_WINDOW
    return pl.pallas_call(
        _swa_kernel,
        grid=(batch, nq),
        in_specs=[pl.BlockSpec((SWA_TILE, qw), lambda b, i: (b * nq + i, 0)),
                  pl.BlockSpec((seq, kw), lambda b, i: (b, 0)),
                  pl.BlockSpec((seq, kw), lambda b, i: (b, 0)),
                  _resident(e_cols), _resident(aug), _layer(sinks, l)],
        out_specs=pl.BlockSpec((SWA_TILE, qw), lambda b, i: (b * nq + i, 0)),
        out_shape=jax.ShapeDtypeStruct(q.shape, BF16),
        scratch_shapes=[pltpu.VMEM((seq, 2 * kw), BF16),
                        pltpu.VMEM((seq // SWA_WINDOW, 2 * VT_ROWS, SWA_WINDOW), BF16),
                        pltpu.VMEM((nwin, 2 * LANES, N_HEADS * SWA_WINDOW), BF16),
                        pltpu.VMEM((2, 2 * SWA_WINDOW, SWA_WINDOW), F32)],
        compiler_params=_params("parallel", "arbitrary"),
        name="swa_attention",
    )(q, k, v, e_cols, aug, sinks)


def _select_bias(gate, blk, j):
    g = jnp.where(blk < j, gate, GATE_FLOOR)
    bias = jnp.where(blk == j, 0.0, MASK_VALUE)
    blk_f = blk.astype(F32)
    for _ in range(MOBA_TOPK):
        mx = jnp.max(g, axis=0, keepdims=True)
        idx = jnp.min(jnp.where(g == mx, blk_f, float(MOBA_COLS)), axis=0, keepdims=True)
        hit = (blk_f == idx) & (mx > GATE_FLOOR)
        bias = jnp.where(hit, 0.0, bias)
        g = jnp.where(hit, GATE_FLOOR, g)
    return bias


def _moba_kernel(q_ref, k_ref, v_ref, e_ref, aug_ref, o_ref, kc_ref, vt_ref, km_ref, qt_ref, cb_ref, *, nblk):
    j = pl.program_id(1)
    L = MOBA_BLOCK
    cols = KV_GROUP * L
    low = _low_lanes()

    @pl.when(j == 0)
    def _():
        k = k_ref[...]
        e = e_ref[...]
        kc_ref[:, 0:LANES] = jnp.where(low, k, e)
        kc_ref[:, LANES:] = jnp.where(low, e, k)
        vt = v_ref[...].astype(F32).T
        ones_row = jnp.where(lax.broadcasted_iota(jnp.int32, (BF16_ROWS, VISIT * L), 0) == 0,
                             1.0, 0.0).astype(BF16)
        for n in range(nblk // VISIT):
            for half in range(2):
                r0 = half * VT_ROWS
                vt_ref[n, r0:r0 + HEAD_DIM, :] = vt[half * HEAD_DIM:(half + 1) * HEAD_DIM,
                                                    n * VISIT * L:(n + 1) * VISIT * L].astype(BF16)
                vt_ref[n, r0 + HEAD_DIM:r0 + VT_ROWS, :] = ones_row
        qt_ref[...] = jnp.zeros(qt_ref.shape, BF16)
        kmean = jnp.mean(k.astype(F32).reshape(nblk, L, LANES), axis=1)
        swapped = pltpu.roll(kmean, HEAD_DIM, axis=1)
        km_ref[...] = jnp.zeros(km_ref.shape, BF16)
        for group in range(2):
            placed = (jnp.where(low, kmean if group == 0 else swapped, 0.0),
                      jnp.where(low, 0.0, swapped if group == 0 else kmean))
            for r in range(2):
                for c, part in enumerate(_split3(placed[r])):
                    r0 = (3 * r + c) * BF16_ROWS
                    km_ref[group, r0:r0 + nblk, :] = part.astype(BF16)
        key = lax.broadcasted_iota(jnp.int32, (L, L), 0)
        qry = lax.broadcasted_iota(jnp.int32, (L, L), 1)
        cb_ref[0] = jnp.zeros((L, L), F32)
        cb_ref[1] = jnp.where(key <= qry, 0.0, MASK_VALUE)

    qpts, gates = [], []
    for tile in range(N_TILES):
        group = (2 * tile) // KV_GROUP
        qpt = q_ref[:, tile * LANES:(tile + 1) * LANES].astype(F32).T
        g = jnp.dot(km_ref[group], qpt.astype(BF16), preferred_element_type=F32)
        for r in range(2):
            r0 = 3 * BF16_ROWS * r
            gates.append(g[r0:r0 + BF16_ROWS] + g[r0 + BF16_ROWS:r0 + 2 * BF16_ROWS]
                         + g[r0 + 2 * BF16_ROWS:r0 + 3 * BF16_ROWS])
        qpts.append(qpt)
    blk = lax.broadcasted_iota(jnp.int32, (MOBA_COLS, N_HEADS * L), 0)
    bias_all = _select_bias(jnp.concatenate(gates, axis=1), blk, j)
    for head in range(N_HEADS):
        group = head // KV_GROUP
        extra = jnp.concatenate([bias_all[:, head * L:(head + 1) * L],
                                 jnp.concatenate([aug_ref[head]] * (L // LANES), axis=1),
                                 jnp.zeros((HEAD_DIM - 2 * MOBA_COLS, L), F32)], axis=0)
        qh = qpts[head // 2][(head % 2) * HEAD_DIM:(head % 2 + 1) * HEAD_DIM]
        rows = [qh, extra] if group == 0 else [extra, qh]
        qt_ref[group * LANES:(group + 1) * LANES, head * L:(head + 1) * L] = (
            jnp.concatenate(rows, axis=0).astype(BF16))

    def visit(n, count, carry):
        m, acc = carry
        cb = jnp.concatenate([cb_ref[(VISIT * n + i == j).astype(jnp.int32)] for i in range(count)], axis=0)
        cb = jnp.concatenate([cb] * N_HEADS, axis=1)
        off = pl.multiple_of(n * (VISIT * L), VISIT * L)
        s = jnp.dot(kc_ref[pl.ds(off, count * L), :], qt_ref[...], preferred_element_type=F32) + cb
        m_new = jnp.maximum(m, jnp.max(s, axis=0, keepdims=True))
        p = jnp.exp2((s - m_new).astype(BF16))
        pv = jnp.concatenate(
            [jnp.dot(vt_ref[n, half * VT_ROWS:(half + 1) * VT_ROWS, 0:count * L], p[:, half * cols:(half + 1) * cols],
                     preferred_element_type=F32) for half in range(2)], axis=1)
        return m_new, jnp.exp2(m - m_new) * acc + pv

    init = (jnp.full((1, 2 * cols), M_INIT, F32), jnp.zeros((VT_ROWS, 2 * cols), F32))
    full = (j + 1) // VISIT
    carry = lax.fori_loop(0, full, lambda n, c: visit(n, VISIT, c), init)
    _, acc = lax.cond((j + 1) % VISIT == 1, lambda c: visit(full, 1, c), lambda c: c, carry)
    for tile in range(N_TILES):
        parts = []
        for r in range(2):
            a = acc[:, (2 * tile + r) * L:(2 * tile + r + 1) * L]
            parts.append(a[0:HEAD_DIM] / a[HEAD_DIM:HEAD_DIM + 1])
        o_ref[:, tile * LANES:(tile + 1) * LANES] = jnp.concatenate(parts, axis=0).T.astype(BF16)


def _moba(q, k, v, e_cols, aug, batch):
    seq = q.shape[0] // batch
    nblk = seq // MOBA_BLOCK
    qw = N_HEADS * HEAD_DIM
    kw = k.shape[1]
    return pl.pallas_call(
        functools.partial(_moba_kernel, nblk=nblk),
        grid=(batch, nblk),
        in_specs=[pl.BlockSpec((MOBA_BLOCK, qw), lambda b, i: (b * nblk + i, 0)),
                  pl.BlockSpec((seq, kw), lambda b, i: (b, 0)),
                  pl.BlockSpec((seq, kw), lambda b, i: (b, 0)),
                  _resident(e_cols), _resident(aug)],
        out_specs=pl.BlockSpec((MOBA_BLOCK, qw), lambda b, i: (b * nblk + i, 0)),
        out_shape=jax.ShapeDtypeStruct(q.shape, BF16),
        scratch_shapes=[pltpu.VMEM((seq, 2 * kw), BF16),
                        pltpu.VMEM((nblk // VISIT, 2 * VT_ROWS, VISIT * MOBA_BLOCK), BF16),
                        pltpu.VMEM((2, 6 * BF16_ROWS, LANES), BF16),
                        pltpu.VMEM((2 * LANES, N_HEADS * MOBA_BLOCK), BF16),
                        pltpu.VMEM((2, MOBA_BLOCK, MOBA_BLOCK), F32)],
        compiler_params=_params("parallel", "arbitrary"),
        name="moba_attention",
    )(q, k, v, e_cols, aug)


def _ssm_kernel(u_ref, wbr_ref, wbi_ref, a_ref, cr_ref, ci_ref, d_ref, wglu_ref, o_ref,
                tb_ref, xr_ref, xi_ref, hr_ref, hi_ref):
    batch, steps, _ = u_ref.shape
    rows = batch * steps
    half_lanes = SSM_LANES // 2
    half_width = SSM_WIDTH // 2

    @pl.when(pl.program_id(0) == 0)
    def _():
        hr_ref[...] = jnp.zeros(hr_ref.shape, F32)
        hi_ref[...] = jnp.zeros(hi_ref.shape, F32)

    tiles = SSM_WIDTH // LANES
    for b in range(batch):
        ub_rows = u_ref[b]
        for c in range(tiles):
            tb_ref[c, pl.ds(b, steps, stride=batch), :] = ub_rows[:, c * LANES:(c + 1) * LANES]
    u = jnp.concatenate([tb_ref[c] for c in range(tiles)], axis=1)
    ub = u.astype(BF16)
    for s in range(2):
        us = ub[:, s * half_width:(s + 1) * half_width]
        xr_ref[:, s * half_lanes:(s + 1) * half_lanes] = jnp.dot(us, wbr_ref[s], preferred_element_type=F32)
        xi_ref[:, s * half_lanes:(s + 1) * half_lanes] = jnp.dot(us, wbi_ref[s], preferred_element_type=F32)

    top = lax.broadcasted_iota(jnp.int32, (SUBLANES, SCAN_LANES), 0) < batch
    for c in range(half_lanes // SCAN_LANES):
        lo = slice(c * SCAN_LANES, (c + 1) * SCAN_LANES)
        hi = slice(half_lanes + c * SCAN_LANES, half_lanes + (c + 1) * SCAN_LANES)
        ar = jnp.where(top, a_ref[0:1, lo], a_ref[0:1, hi])
        ai = jnp.where(top, a_ref[1:2, lo], a_ref[1:2, hi])

        def fold(x_lo, x_hi):
            return (jnp.where(top, x_lo, pltpu.roll(x_hi, batch, axis=0)),
                    jnp.where(top, pltpu.roll(x_lo, batch, axis=0), x_hi))

        def body(k, carry, lo=lo, hi=hi, ar=ar, ai=ai):
            pr, pi = carry
            rws = pl.ds(pl.multiple_of(k * SUBLANES, SUBLANES), SUBLANES)
            x1r, x2r = fold(xr_ref[rws, lo], xr_ref[rws, hi])
            x1i, x2i = fold(xi_ref[rws, lo], xi_ref[rws, hi])
            h1r = ar * pr - ai * pi + x1r
            h1i = ar * pi + ai * pr + x1i
            h2r = ar * h1r - ai * h1i + x2r
            h2i = ar * h1i + ai * h1r + x2i
            xr_ref[rws, lo], xr_ref[rws, hi] = fold(h1r, h2r)
            xi_ref[rws, lo], xi_ref[rws, hi] = fold(h1i, h2i)
            return h2r, h2i

        pr, pi = lax.fori_loop(0, rows // SUBLANES, body, (hr_ref[:, lo], hi_ref[:, lo]))
        hr_ref[:, lo] = pr
        hi_ref[:, lo] = pi

    ys = []
    for s in range(2):
        hr = xr_ref[:, s * half_lanes:(s + 1) * half_lanes].astype(BF16)
        hi = xi_ref[:, s * half_lanes:(s + 1) * half_lanes].astype(BF16)
        ys.append(jnp.dot(hr, cr_ref[s], preferred_element_type=F32)
                  + jnp.dot(hi, ci_ref[s], preferred_element_type=F32))
    y = jnp.concatenate(ys, axis=1) + d_ref[...] * u
    z = jnp.dot(jax.nn.gelu(y).astype(BF16), wglu_ref[...], preferred_element_type=F32)
    out = z[:, :SSM_WIDTH] * jax.nn.sigmoid(z[:, SSM_WIDTH:])
    for c in range(tiles):
        tb_ref[c] = out[:, c * LANES:(c + 1) * LANES]
    for b in range(batch):
        o_ref[b] = jnp.concatenate([tb_ref[c, pl.ds(b, steps, stride=batch), :] for c in range(tiles)],
                                   axis=1).astype(BF16)


def _ssm(u, wbr, wbi, a, cr, ci, d, wglu, batch, l):
    seq = u.shape[0] // batch
    rows = SSM_CHUNK * batch
    block = pl.BlockSpec((batch, SSM_CHUNK, SSM_WIDTH), lambda i: (0, i, 0))
    out = pl.pallas_call(
        _ssm_kernel,
        grid=(seq // SSM_CHUNK,),
        in_specs=[block] + [_layer(t, l) for t in (wbr, wbi, a, cr, ci, d, wglu)],
        out_specs=block,
        out_shape=jax.ShapeDtypeStruct((batch, seq, SSM_WIDTH), BF16),
        scratch_shapes=[pltpu.VMEM((SSM_WIDTH // LANES, rows, LANES), F32),
                        pltpu.VMEM((rows, SSM_LANES), F32), pltpu.VMEM((rows, SSM_LANES), F32),
                        pltpu.VMEM((SUBLANES, SSM_LANES // 2), F32), pltpu.VMEM((SUBLANES, SSM_LANES // 2), F32)],
        compiler_params=_params("arbitrary"),
        name="s5_ssm",
    )(u.reshape(batch, seq, SSM_WIDTH), wbr, wbi, a, cr, ci, d, wglu)
    return out.reshape(batch * seq, SSM_WIDTH)


def _merge_kernel(x_ref, oa_ref, gl_ref, om_ref, g_ref, wga_ref, wgs_ref, wgm_ref, woa_ref, wos_ref, wom_ref,
                  wout_ref, o_ref):
    x = x_ref[...]
    h = _rms(x, g_ref[...]).astype(BF16)
    mixed = None
    for b_ref, wg_ref, w_ref in ((oa_ref, wga_ref, woa_ref), (gl_ref, wgs_ref, wos_ref), (om_ref, wgm_ref, wom_ref)):
        gate = jax.nn.sigmoid(jnp.dot(h, wg_ref[...], preferred_element_type=F32))
        term = gate * jnp.dot(b_ref[...], w_ref[...], preferred_element_type=F32)
        mixed = term if mixed is None else mixed + term
    o_ref[...] = x + jnp.dot(mixed.astype(BF16), wout_ref[...], preferred_element_type=F32)


def _merge(x, oa, gl, om, gain, w_in, woa, wos, wom, wout, l):
    rows, d = x.shape
    tile = lambda n: pl.BlockSpec((ROW_TILE, n), lambda i: (i, 0))
    first = _PROJ_WIDTH // d
    gates = [pl.BlockSpec((None, d, d), lambda i, n=n: (l, 0, first + n)) for n in range(3)]
    return pl.pallas_call(
        _merge_kernel,
        grid=(rows // ROW_TILE,),
        in_specs=[tile(d), tile(oa.shape[1]), tile(gl.shape[1]), tile(om.shape[1]), _layer(gain, l)] + gates
                 + [_layer(t, l) for t in (woa, wos, wom, wout)],
        out_specs=tile(d),
        out_shape=jax.ShapeDtypeStruct((rows, d), F32),
        compiler_params=_params("parallel"),
        name="merge",
    )(x, oa, gl, om, gain, w_in, w_in, w_in, woa, wos, wom, wout)


def _ffn_kernel(x_ref, g_ref, w1_ref, w2_ref, gf_ref, o_ref, *, final_norm):
    x = x_ref[...]
    d = x.shape[1]
    h = _rms(x, g_ref[...]).astype(BF16)
    acc = x
    for c in range(w1_ref.shape[1] // d):
        a = jnp.maximum(jnp.dot(h, w1_ref[:, c * d:(c + 1) * d], preferred_element_type=F32), 0.0)
        acc = acc + jnp.dot((a * a).astype(BF16), w2_ref[c * d:(c + 1) * d, :], preferred_element_type=F32)
    o_ref[...] = _rms(acc, gf_ref[...]) if final_norm else acc


def _ffn(x, gain, w1, w2, gain_final, l, final_norm):
    rows, d = x.shape
    return pl.pallas_call(
        functools.partial(_ffn_kernel, final_norm=final_norm),
        grid=(rows // ROW_TILE,),
        in_specs=[pl.BlockSpec((ROW_TILE, d), lambda i: (i, 0)),
                  _layer(gain, l), _layer(w1, l), _layer(w2, l), _resident(gain_final)],
        out_specs=pl.BlockSpec((ROW_TILE, d), lambda i: (i, 0)),
        out_shape=jax.ShapeDtypeStruct((rows, d), F32),
        compiler_params=_params("parallel"),
        name="ffn",
    )(x, gain, w1, w2, gain_final)


def _alibi_slopes():
    n = 2 * N_HEADS
    return np.asarray(2.0 ** (-ALIBI_MAX_BIAS * np.arange(1, n + 1) / n), np.float32)


def _swa_constants(slopes):
    w = SWA_WINDOW
    s_rel = np.arange(2 * w, dtype=np.float32) - w
    e = np.zeros((2 * w, 2 * LANES), np.float32)
    for o in (HEAD_DIM, LANES):
        e[:, o:o + 3] = s_rel[:, None]
        e[:, o + 3:o + 6] = 1.0
    t_rel = np.arange(w, dtype=np.float32)
    sl = slopes * np.float32(LOG2E)
    aug = np.zeros((N_HEADS, HEAD_DIM, w), np.float32)
    for c, part in enumerate(_split3(sl)):
        aug[:, c, :] = part[:, None]
    for c, part in enumerate(_split3(-sl[:, None] * t_rel[None, :])):
        aug[:, 3 + c, :] = part
    return e.astype(BF16), aug


def _moba_constants(slopes, seq):
    assert seq // MOBA_BLOCK <= MOBA_COLS and (seq // MOBA_BLOCK) % VISIT == 0
    pos = np.arange(seq)
    blk = pos // MOBA_BLOCK
    off = (pos % MOBA_BLOCK).astype(np.float32)
    base = (blk * MOBA_BLOCK).astype(np.float32)
    n = MOBA_COLS
    onehot = (blk[:, None] == np.arange(n)[None, :]).astype(np.float32)
    e = np.zeros((seq, LANES), np.float32)
    for o in (MOBA_HI, 0):
        e[:, o:o + n] = onehot
        e[:, o + n:o + n + 3] = off[:, None]
        e[:, o + n + 3:o + n + 6] = base[:, None]
    aug = np.zeros((N_HEADS, MOBA_COLS, LANES), np.float32)
    for c, part in enumerate(_split3(slopes * np.float32(LOG2E))):
        aug[:, c, :] = part[:, None]
        aug[:, 3 + c, :] = part[:, None]
    return e.astype(BF16), aug


def _ssm_weights(lam_re, lam_im, log_step, b_re, b_im, c_re, c_im):
    lr, li = lam_re.astype(F32), lam_im.astype(F32)
    dt = jnp.exp(log_step.astype(F32))[:, None]
    mag = jnp.exp(lr * dt)
    ab_re, ab_im = mag * jnp.cos(li * dt), mag * jnp.sin(li * dt)
    den = lr * lr + li * li
    nr, ni = ab_re - 1.0, ab_im
    f_re, f_im = (nr * lr + ni * li) / den, (ni * lr - nr * li) / den
    br, bi = b_re.astype(F32), b_im.astype(F32)
    bb_re = f_re[..., None] * br - f_im[..., None] * bi
    bb_im = f_re[..., None] * bi + f_im[..., None] * br
    half = SSM_GROUPS // 2
    row_group = jnp.arange(half * SSM_GROUP) // SSM_GROUP
    lane_group = jnp.arange(half * SSM_STATE) // SSM_STATE

    def in_map(bb):
        rows = jnp.swapaxes(bb, 1, 2).reshape(2, half * SSM_GROUP, SSM_STATE)
        return jnp.where(row_group[:, None] == lane_group[None, :], jnp.tile(rows, (1, 1, half)), 0.0)

    def out_map(c):
        rows = jnp.swapaxes(c, 1, 2).reshape(2, half * SSM_STATE, SSM_GROUP)
        return jnp.where(lane_group[:, None] == row_group[None, :], jnp.tile(rows, (1, 1, half)), 0.0)

    a = jnp.stack([ab_re.reshape(-1), ab_im.reshape(-1)])
    return (in_map(bb_re).astype(BF16), in_map(bb_im).astype(BF16), a,
            out_map(c_re.astype(F32)).astype(BF16), out_map(-c_im.astype(F32)).astype(BF16))


def kernel(x, norm_mix, w_in, sinks, lam_re, lam_im, log_step, b_re, b_im, c_re, c_im, d_skip, w_glu,
           w_o_swa, w_o_ssm, w_o_moba, w_out, norm_ffn, w_ff1, w_ff2, norm_final):
    batch, seq, d = x.shape
    depth = w_in.shape[0]
    assert batch * 2 == SUBLANES, "the S5 scan packs two time steps of all batches into one vreg"
    slopes = _alibi_slopes()
    swa_e, swa_aug = _swa_constants(slopes[:N_HEADS])
    moba_e, moba_aug = _moba_constants(slopes[N_HEADS:], seq)

    w_all = w_in.astype(BF16)
    gain_mix = norm_mix.reshape(depth, 1, d)
    gain_ffn = norm_ffn.reshape(depth, 1, d)
    gain_final = norm_final.reshape(1, d)
    sink_tab = jnp.repeat(sinks.astype(F32) * LOG2E, SWA_WINDOW, axis=1).reshape(depth, 1, N_HEADS * SWA_WINDOW)
    wbr, wbi, a, cr, ci = jax.vmap(_ssm_weights)(lam_re, lam_im, log_step, b_re, b_im, c_re, c_im)
    skip = d_skip.astype(F32).reshape(depth, 1, SSM_WIDTH)
    wglu = w_glu.astype(BF16)
    woa, wos, wom = w_o_swa.astype(BF16), w_o_ssm.astype(BF16), w_o_moba.astype(BF16)
    wout, w1, w2 = w_out.astype(BF16), w_ff1.astype(BF16), w_ff2.astype(BF16)

    xs = x.reshape(batch * seq, d)
    for l in range(depth):
        qa, ka, va, u, qm, km, vm = _norm_proj(xs, gain_mix, w_all, l)
        oa = _swa(qa, ka, va, swa_e, swa_aug, sink_tab, batch, l)
        gl = _ssm(u, wbr, wbi, a, cr, ci, skip, wglu, batch, l)
        om = _moba(qm, km, vm, moba_e, moba_aug, batch)
        xs = _merge(xs, oa, gl, om, gain_mix, w_all, woa, wos, wom, wout, l)
        xs = _ffn(xs, gain_ffn, w1, w2, gain_final, l, final_norm=(l == depth - 1))
    return xs.reshape(batch, seq, d)
```

```python
import functools

import numpy as np
import jax
import jax.numpy as jnp
from jax import lax
from jax.experimental import pallas as pl
from jax.experimental.pallas import tpu as pltpu

F32 = jnp.float32
BF16 = jnp.bfloat16

HEAD_DIM = 64
N_HEADS = 8
KV_GROUP = 4
N_TILES = N_HEADS // 2
ATTN_SCALE = HEAD_DIM ** -0.5
LOG2E = 1.4426950408889634
SWA_WINDOW = 128
MOBA_BLOCK = 256
MOBA_TOPK = 3
SSM_GROUPS = 32
SSM_GROUP = 16
SSM_STATE = 64
SSM_WIDTH = SSM_GROUPS * SSM_GROUP
SSM_LANES = SSM_GROUPS * SSM_STATE
NORM_EPS = 1e-6
ALIBI_MAX_BIAS = 8.0
MASK_VALUE = -1e30
M_INIT = 0.5 * MASK_VALUE
GATE_FLOOR = -3.0e38

LANES = 128
SUBLANES = 8
BF16_ROWS = 16
VMEM_LIMIT = 56 * 1024 * 1024

ROW_TILE = 1024
SSM_CHUNK = 256
SWA_TILE = 1024
SWA_LOOKAHEAD = 3
SCAN_LANES = 512
MOBA_COLS = 2 * SUBLANES
MOBA_HI = HEAD_DIM
VT_ROWS = HEAD_DIM + BF16_ROWS
VISIT = 2


def _params(*sem):
    return pltpu.CompilerParams(dimension_semantics=sem, vmem_limit_bytes=VMEM_LIMIT)


def _resident(arr):
    zeros = (0,) * arr.ndim
    return pl.BlockSpec(arr.shape, lambda *_: zeros)


def _layer(arr, l):
    zeros = (0,) * (arr.ndim - 1)
    return pl.BlockSpec((None,) + arr.shape[1:], lambda *_: (l,) + zeros)


def _rms(x, gain):
    return x * lax.rsqrt(jnp.mean(x * x, axis=-1, keepdims=True) + NORM_EPS) * gain


def _split3(x):
    hi = x.astype(BF16).astype(F32)
    mid = (x - hi).astype(BF16).astype(F32)
    lo = (x - hi - mid).astype(BF16).astype(F32)
    return hi, mid, lo


def _low_lanes():
    return lax.broadcasted_iota(jnp.int32, (1, LANES), 1) < HEAD_DIM


_PROJ_SIZES = (512, 128, 128, 512, 512, 128, 128)
_PROJ_WIDTH = sum(_PROJ_SIZES)
_Q_SCALE = ATTN_SCALE * LOG2E


def _norm_proj_kernel(x_ref, g_ref, w_ref, qa_ref, ka_ref, va_ref, u_ref, qm_ref, km_ref, vm_ref):
    h = _rms(x_ref[...], g_ref[...]).astype(BF16)
    proj = jnp.dot(h, w_ref[...], preferred_element_type=F32)
    start = 0
    for ref, size in zip((qa_ref, ka_ref, va_ref, u_ref, qm_ref, km_ref, vm_ref), _PROJ_SIZES):
        part = proj[:, start:start + size]
        if ref is qa_ref or ref is qm_ref:
            part = part * _Q_SCALE
        ref[...] = part.astype(ref.dtype)
        start += size


def _norm_proj(x, gain, w, l):
    rows, d = x.shape
    dts = (BF16,) * len(_PROJ_SIZES)
    return pl.pallas_call(
        _norm_proj_kernel,
        grid=(rows // ROW_TILE,),
        in_specs=[pl.BlockSpec((ROW_TILE, d), lambda i: (i, 0)), _layer(gain, l),
                  pl.BlockSpec((None, d, _PROJ_WIDTH), lambda i: (l, 0, 0))],
        out_specs=[pl.BlockSpec((ROW_TILE, n), lambda i: (i, 0)) for n in _PROJ_SIZES],
        out_shape=[jax.ShapeDtypeStruct((rows, n), dt) for n, dt in zip(_PROJ_SIZES, dts)],
        compiler_params=_params("parallel"),
        name="norm_proj",
    )(x, gain, w)


def _swa_kernel(q_ref, k_ref, v_ref, e_ref, aug_ref, sink_ref, o_ref, kc_ref, vt_ref, qt_ref, mb_ref):
    qi = pl.program_id(1)
    w = SWA_WINDOW
    half_cols = KV_GROUP * w
    low = _low_lanes()

    @pl.when(qi == 0)
    def _():
        k = k_ref[...]
        zero = jnp.zeros_like(k)
        kc_ref[:, 0:LANES] = jnp.where(low, k, zero)
        kc_ref[:, LANES:] = jnp.where(low, zero, k)
        vt = v_ref[...].astype(F32).T
        ones_row = jnp.where(lax.broadcasted_iota(jnp.int32, (BF16_ROWS, w), 0) == 0, 1.0, 0.0).astype(BF16)
        for n in range(vt_ref.shape[0]):
            for half in range(2):
                r0 = half * VT_ROWS
                vt_ref[n, r0:r0 + HEAD_DIM, :] = vt[half * HEAD_DIM:(half + 1) * HEAD_DIM,
                                                    n * w:(n + 1) * w].astype(BF16)
                vt_ref[n, r0 + HEAD_DIM:r0 + VT_ROWS, :] = ones_row
        qt_ref[...] = jnp.zeros(qt_ref.shape, BF16)
        key = lax.broadcasted_iota(jnp.int32, (2 * w, w), 0)
        qry = lax.broadcasted_iota(jnp.int32, (2 * w, w), 1)
        band = (key > qry) & (key <= qry + w)
        mb_ref[0] = jnp.where(band, 0.0, MASK_VALUE)
        mb_ref[1] = jnp.where(band & (key >= w), 0.0, MASK_VALUE)

    lane2 = lax.broadcasted_iota(jnp.int32, (1, 2 * LANES), 1)
    e_lanes = (lane2 >= HEAD_DIM) & (lane2 < 2 * LANES - HEAD_DIM)
    e_cols = e_ref[...]
    sink = sink_ref[...]
    def scores(blk):
        t0 = qi * SWA_TILE + blk * w
        prev = jnp.maximum(t0 - w, 0)
        kwin = jnp.concatenate([kc_ref[pl.ds(pl.multiple_of(prev, w), w), :],
                                kc_ref[pl.ds(pl.multiple_of(t0, w), w), :]], axis=0)
        kcat = jnp.where(e_lanes, e_cols, kwin)
        vtw = jnp.concatenate([vt_ref[prev // w], vt_ref[t0 // w]], axis=1)
        for tile in range(N_TILES):
            qpt = q_ref[blk * w:(blk + 1) * w, tile * LANES:(tile + 1) * LANES].astype(F32).T
            for r in range(2):
                head = 2 * tile + r
                qh = qpt[r * HEAD_DIM:(r + 1) * HEAD_DIM]
                group = head // KV_GROUP
                rows = [qh, aug_ref[head]] if group == 0 else [aug_ref[head], qh]
                qt_ref[blk, group * LANES:(group + 1) * LANES, head * w:(head + 1) * w] = (
                    jnp.concatenate(rows, axis=0).astype(BF16))
        mb = mb_ref[(t0 == 0).astype(jnp.int32)]
        s = jnp.dot(kcat, qt_ref[blk], preferred_element_type=F32) + jnp.concatenate([mb] * N_HEADS, axis=1)
        return s, jnp.max(s, axis=0, keepdims=True), vtw

    nwin = SWA_TILE // w
    ahead = [scores(b) for b in range(SWA_LOOKAHEAD)]
    for blk in range(nwin):
        s, smax, vtw = ahead.pop(0)
        m = jnp.maximum(smax, sink)
        if blk + SWA_LOOKAHEAD < nwin:
            ahead.append(scores(blk + SWA_LOOKAHEAD))
            m = jnp.maximum(m, jnp.minimum(m, ahead[-1][1]))
        p = jnp.exp2((s - m).astype(BF16))
        pv = jnp.concatenate(
            [jnp.dot(vtw[half * VT_ROWS:(half + 1) * VT_ROWS], p[:, half * half_cols:(half + 1) * half_cols],
                     preferred_element_type=F32) for half in range(2)], axis=1)
        o_t = pv[0:HEAD_DIM] / (pv[HEAD_DIM:HEAD_DIM + 1] + jnp.exp2(sink - m))
        for tile in range(N_TILES):
            both = jnp.concatenate([o_t[:, (2 * tile + r) * w:(2 * tile + r + 1) * w] for r in range(2)], axis=0)
            o_ref[blk * w:(blk + 1) * w, tile * LANES:(tile + 1) * LANES] = both.T.astype(BF16)


def _swa(q, k, v, e_cols, aug, sinks, batch, l):
    seq = q.shape[0] // batch
    qw = N_HEADS * HEAD_DIM
    kw = k.shape[1]
    nq = seq // SWA_TILE
    nwin = SWA_TILE // SWA_WINDOW
    return pl.pallas_call(
        _swa_kernel,
        grid=(batch, nq),
        in_specs=[pl.BlockSpec((SWA_TILE, qw), lambda b, i: (b * nq + i, 0)),
                  pl.BlockSpec((seq, kw), lambda b, i: (b, 0)),
                  pl.BlockSpec((seq, kw), lambda b, i: (b, 0)),
                  _resident(e_cols), _resident(aug), _layer(sinks, l)],
        out_specs=pl.BlockSpec((SWA_TILE, qw), lambda b, i: (b * nq + i, 0)),
        out_shape=jax.ShapeDtypeStruct(q.shape, BF16),
        scratch_shapes=[pltpu.VMEM((seq, 2 * kw), BF16),
                        pltpu.VMEM((seq // SWA_WINDOW, 2 * VT_ROWS, SWA_WINDOW), BF16),
                        pltpu.VMEM((nwin, 2 * LANES, N_HEADS * SWA_WINDOW), BF16),
                        pltpu.VMEM((2, 2 * SWA_WINDOW, SWA_WINDOW), F32)],
        compiler_params=_params("parallel", "arbitrary"),
        name="swa_attention",
    )(q, k, v, e_cols, aug, sinks)


def _select_bias(gate, blk, j):
    g = jnp.where(blk < j, gate, GATE_FLOOR)
    bias = jnp.where(blk == j, 0.0, MASK_VALUE)
    blk_f = blk.astype(F32)
    for _ in range(MOBA_TOPK):
        mx = jnp.max(g, axis=0, keepdims=True)
        idx = jnp.min(jnp.where(g == mx, blk_f, float(MOBA_COLS)), axis=0, keepdims=True)
        hit = (blk_f == idx) & (mx > GATE_FLOOR)
        bias = jnp.where(hit, 0.0, bias)
        g = jnp.where(hit, GATE_FLOOR, g)
    return bias


def _moba_kernel(q_ref, k_ref, v_ref, e_ref, aug_ref, o_ref, kc_ref, vt_ref, km_ref, qt_ref, cb_ref, *, nblk):
    j = pl.program_id(1)
    L = MOBA_BLOCK
    cols = KV_GROUP * L
    low = _low_lanes()

    @pl.when(j == 0)
    def _():
        k = k_ref[...]
        e = e_ref[...]
        kc_ref[:, 0:LANES] = jnp.where(low, k, e)
        kc_ref[:, LANES:] = jnp.where(low, e, k)
        vt = v_ref[...].astype(F32).T
        ones_row = jnp.where(lax.broadcasted_iota(jnp.int32, (BF16_ROWS, VISIT * L), 0) == 0,
                             1.0, 0.0).astype(BF16)
        for n in range(nblk // VISIT):
            for half in range(2):
                r0 = half * VT_ROWS
                vt_ref[n, r0:r0 + HEAD_DIM, :] = vt[half * HEAD_DIM:(half + 1) * HEAD_DIM,
                                                    n * VISIT * L:(n + 1) * VISIT * L].astype(BF16)
                vt_ref[n, r0 + HEAD_DIM:r0 + VT_ROWS, :] = ones_row
        qt_ref[...] = jnp.zeros(qt_ref.shape, BF16)
        kmean = jnp.mean(k.astype(F32).reshape(nblk, L, LANES), axis=1)
        swapped = pltpu.roll(kmean, HEAD_DIM, axis=1)
        km_ref[...] = jnp.zeros(km_ref.shape, BF16)
        for group in range(2):
            placed = (jnp.where(low, kmean if group == 0 else swapped, 0.0),
                      jnp.where(low, 0.0, swapped if group == 0 else kmean))
            for r in range(2):
                for c, part in enumerate(_split3(placed[r])):
                    r0 = (3 * r + c) * BF16_ROWS
                    km_ref[group, r0:r0 + nblk, :] = part.astype(BF16)
        key = lax.broadcasted_iota(jnp.int32, (L, L), 0)
        qry = lax.broadcasted_iota(jnp.int32, (L, L), 1)
        cb_ref[0] = jnp.zeros((L, L), F32)
        cb_ref[1] = jnp.where(key <= qry, 0.0, MASK_VALUE)

    qpts, gates = [], []
    for tile in range(N_TILES):
        group = (2 * tile) // KV_GROUP
        qpt = q_ref[:, tile * LANES:(tile + 1) * LANES].astype(F32).T
        g = jnp.dot(km_ref[group], qpt.astype(BF16), preferred_element_type=F32)
        for r in range(2):
            r0 = 3 * BF16_ROWS * r
            gates.append(g[r0:r0 + BF16_ROWS] + g[r0 + BF16_ROWS:r0 + 2 * BF16_ROWS]
                         + g[r0 + 2 * BF16_ROWS:r0 + 3 * BF16_ROWS])
        qpts.append(qpt)
    blk = lax.broadcasted_iota(jnp.int32, (MOBA_COLS, N_HEADS * L), 0)
    bias_all = _select_bias(jnp.concatenate(gates, axis=1), blk, j)
    for head in range(N_HEADS):
        group = head // KV_GROUP
        extra = jnp.concatenate([bias_all[:, head * L:(head + 1) * L],
                                 jnp.concatenate([aug_ref[head]] * (L // LANES), axis=1),
                                 jnp.zeros((HEAD_DIM - 2 * MOBA_COLS, L), F32)], axis=0)
        qh = qpts[head // 2][(head % 2) * HEAD_DIM:(head % 2 + 1) * HEAD_DIM]
        rows = [qh, extra] if group == 0 else [extra, qh]
        qt_ref[group * LANES:(group + 1) * LANES, head * L:(head + 1) * L] = (
            jnp.concatenate(rows, axis=0).astype(BF16))

    def visit(n, count, carry):
        m, acc = carry
        cb = jnp.concatenate([cb_ref[(VISIT * n + i == j).astype(jnp.int32)] for i in range(count)], axis=0)
        cb = jnp.concatenate([cb] * N_HEADS, axis=1)
        off = pl.multiple_of(n * (VISIT * L), VISIT * L)
        s = jnp.dot(kc_ref[pl.ds(off, count * L), :], qt_ref[...], preferred_element_type=F32) + cb
        m_new = jnp.maximum(m, jnp.max(s, axis=0, keepdims=True))
        p = jnp.exp2((s - m_new).astype(BF16))
        pv = jnp.concatenate(
            [jnp.dot(vt_ref[n, half * VT_ROWS:(half + 1) * VT_ROWS, 0:count * L], p[:, half * cols:(half + 1) * cols],
                     preferred_element_type=F32) for half in range(2)], axis=1)
        return m_new, jnp.exp2(m - m_new) * acc + pv

    init = (jnp.full((1, 2 * cols), M_INIT, F32), jnp.zeros((VT_ROWS, 2 * cols), F32))
    full = (j + 1) // VISIT
    carry = lax.fori_loop(0, full, lambda n, c: visit(n, VISIT, c), init)
    _, acc = lax.cond((j + 1) % VISIT == 1, lambda c: visit(full, 1, c), lambda c: c, carry)
    for tile in range(N_TILES):
        parts = []
        for r in range(2):
            a = acc[:, (2 * tile + r) * L:(2 * tile + r + 1) * L]
            parts.append(a[0:HEAD_DIM] / a[HEAD_DIM:HEAD_DIM + 1])
        o_ref[:, tile * LANES:(tile + 1) * LANES] = jnp.concatenate(parts, axis=0).T.astype(BF16)


def _moba(q, k, v, e_cols, aug, batch):
    seq = q.shape[0] // batch
    nblk = seq // MOBA_BLOCK
    qw = N_HEADS * HEAD_DIM
    kw = k.shape[1]
    return pl.pallas_call(
        functools.partial(_moba_kernel, nblk=nblk),
        grid=(batch, nblk),
        in_specs=[pl.BlockSpec((MOBA_BLOCK, qw), lambda b, i: (b * nblk + i, 0)),
                  pl.BlockSpec((seq, kw), lambda b, i: (b, 0)),
                  pl.BlockSpec((seq, kw), lambda b, i: (b, 0)),
                  _resident(e_cols), _resident(aug)],
        out_specs=pl.BlockSpec((MOBA_BLOCK, qw), lambda b, i: (b * nblk + i, 0)),
        out_shape=jax.ShapeDtypeStruct(q.shape, BF16),
        scratch_shapes=[pltpu.VMEM((seq, 2 * kw), BF16),
                        pltpu.VMEM((nblk // VISIT, 2 * VT_ROWS, VISIT * MOBA_BLOCK), BF16),
                        pltpu.VMEM((2, 6 * BF16_ROWS, LANES), BF16),
                        pltpu.VMEM((2 * LANES, N_HEADS * MOBA_BLOCK), BF16),
                        pltpu.VMEM((2, MOBA_BLOCK, MOBA_BLOCK), F32)],
        compiler_params=_params("parallel", "arbitrary"),
        name="moba_attention",
    )(q, k, v, e_cols, aug)


def _ssm_kernel(u_ref, wbr_ref, wbi_ref, a_ref, cr_ref, ci_ref, d_ref, wglu_ref, o_ref,
                tb_ref, xr_ref, xi_ref, hr_ref, hi_ref):
    batch, steps, _ = u_ref.shape
    rows = batch * steps
    half_lanes = SSM_LANES // 2
    half_width = SSM_WIDTH // 2

    @pl.when(pl.program_id(0) == 0)
    def _():
        hr_ref[...] = jnp.zeros(hr_ref.shape, F32)
        hi_ref[...] = jnp.zeros(hi_ref.shape, F32)

    tiles = SSM_WIDTH // LANES
    for b in range(batch):
        ub_rows = u_ref[b].astype(F32)
        for c in range(tiles):
            tb_ref[c, pl.ds(b, steps, stride=batch), :] = ub_rows[:, c * LANES:(c + 1) * LANES]
    u = jnp.concatenate([tb_ref[c] for c in range(tiles)], axis=1)
    ub = u.astype(BF16)
    for s in range(2):
        us = ub[:, s * half_width:(s + 1) * half_width]
        xr_ref[:, s * half_lanes:(s + 1) * half_lanes] = jnp.dot(us, wbr_ref[s], preferred_element_type=F32)
        xi_ref[:, s * half_lanes:(s + 1) * half_lanes] = jnp.dot(us, wbi_ref[s], preferred_element_type=F32)

    top = lax.broadcasted_iota(jnp.int32, (SUBLANES, SCAN_LANES), 0) < batch
    for c in range(half_lanes // SCAN_LANES):
        lo = slice(c * SCAN_LANES, (c + 1) * SCAN_LANES)
        hi = slice(half_lanes + c * SCAN_LANES, half_lanes + (c + 1) * SCAN_LANES)
        ar = jnp.where(top, a_ref[0:1, lo], a_ref[0:1, hi])
        ai = jnp.where(top, a_ref[1:2, lo], a_ref[1:2, hi])

        def fold(x_lo, x_hi):
            return (jnp.where(top, x_lo, pltpu.roll(x_hi, batch, axis=0)),
                    jnp.where(top, pltpu.roll(x_lo, batch, axis=0), x_hi))

        def body(k, carry, lo=lo, hi=hi, ar=ar, ai=ai):
            pr, pi = carry
            rws = pl.ds(pl.multiple_of(k * SUBLANES, SUBLANES), SUBLANES)
            x1r, x2r = fold(xr_ref[rws, lo], xr_ref[rws, hi])
            x1i, x2i = fold(xi_ref[rws, lo], xi_ref[rws, hi])
            h1r = ar * pr - ai * pi + x1r
            h1i = ar * pi + ai * pr + x1i
            h2r = ar * h1r - ai * h1i + x2r
            h2i = ar * h1i + ai * h1r + x2i
            xr_ref[rws, lo], xr_ref[rws, hi] = fold(h1r, h2r)
            xi_ref[rws, lo], xi_ref[rws, hi] = fold(h1i, h2i)
            return h2r, h2i

        pr, pi = lax.fori_loop(0, rows // SUBLANES, body, (hr_ref[:, lo], hi_ref[:, lo]))
        hr_ref[:, lo] = pr
        hi_ref[:, lo] = pi

    ys = []
    for s in range(2):
        hr = xr_ref[:, s * half_lanes:(s + 1) * half_lanes].astype(BF16)
        hi = xi_ref[:, s * half_lanes:(s + 1) * half_lanes].astype(BF16)
        ys.append(jnp.dot(hr, cr_ref[s], preferred_element_type=F32)
                  + jnp.dot(hi, ci_ref[s], preferred_element_type=F32))
    y = jnp.concatenate(ys, axis=1) + d_ref[...] * u
    z = jnp.dot(jax.nn.gelu(y).astype(BF16), wglu_ref[...], preferred_element_type=F32)
    out = z[:, :SSM_WIDTH] * jax.nn.sigmoid(z[:, SSM_WIDTH:])
    for c in range(tiles):
        tb_ref[c] = out[:, c * LANES:(c + 1) * LANES]
    for b in range(batch):
        o_ref[b] = jnp.concatenate([tb_ref[c, pl.ds(b, steps, stride=batch), :] for c in range(tiles)],
                                   axis=1).astype(BF16)


def _ssm(u, wbr, wbi, a, cr, ci, d, wglu, batch, l):
    seq = u.shape[0] // batch
    rows = SSM_CHUNK * batch
    block = pl.BlockSpec((batch, SSM_CHUNK, SSM_WIDTH), lambda i: (0, i, 0))
    out = pl.pallas_call(
        _ssm_kernel,
        grid=(seq // SSM_CHUNK,),
        in_specs=[block] + [_layer(t, l) for t in (wbr, wbi, a, cr, ci, d, wglu)],
        out_specs=block,
        out_shape=jax.ShapeDtypeStruct((batch, seq, SSM_WIDTH), BF16),
        scratch_shapes=[pltpu.VMEM((SSM_WIDTH // LANES, rows, LANES), F32),
                        pltpu.VMEM((rows, SSM_LANES), F32), pltpu.VMEM((rows, SSM_LANES), F32),
                        pltpu.VMEM((SUBLANES, SSM_LANES // 2), F32), pltpu.VMEM((SUBLANES, SSM_LANES // 2), F32)],
        compiler_params=_params("arbitrary"),
        name="s5_ssm",
    )(u.reshape(batch, seq, SSM_WIDTH), wbr, wbi, a, cr, ci, d, wglu)
    return out.reshape(batch * seq, SSM_WIDTH)


def _merge_kernel(x_ref, oa_ref, gl_ref, om_ref, g_ref, wga_ref, wgs_ref, wgm_ref, woa_ref, wos_ref, wom_ref,
                  wout_ref, o_ref):
    x = x_ref[...]
    h = _rms(x, g_ref[...]).astype(BF16)
    mixed = None
    for b_ref, wg_ref, w_ref in ((oa_ref, wga_ref, woa_ref), (gl_ref, wgs_ref, wos_ref), (om_ref, wgm_ref, wom_ref)):
        gate = jax.nn.sigmoid(jnp.dot(h, wg_ref[...], preferred_element_type=F32))
        term = gate * jnp.dot(b_ref[...], w_ref[...], preferred_element_type=F32)
        mixed = term if mixed is None else mixed + term
    o_ref[...] = x + jnp.dot(mixed.astype(BF16), wout_ref[...], preferred_element_type=F32)


def _merge(x, oa, gl, om, gain, w_in, woa, wos, wom, wout, l):
    rows, d = x.shape
    tile = lambda n: pl.BlockSpec((ROW_TILE, n), lambda i: (i, 0))
    first = _PROJ_WIDTH // d
    gates = [pl.BlockSpec((None, d, d), lambda i, n=n: (l, 0, first + n)) for n in range(3)]
    return pl.pallas_call(
        _merge_kernel,
        grid=(rows // ROW_TILE,),
        in_specs=[tile(d), tile(oa.shape[1]), tile(gl.shape[1]), tile(om.shape[1]), _layer(gain, l)] + gates
                 + [_layer(t, l) for t in (woa, wos, wom, wout)],
        out_specs=tile(d),
        out_shape=jax.ShapeDtypeStruct((rows, d), F32),
        compiler_params=_params("parallel"),
        name="merge",
    )(x, oa, gl, om, gain, w_in, w_in, w_in, woa, wos, wom, wout)


def _ffn_kernel(x_ref, g_ref, w1_ref, w2_ref, gf_ref, o_ref, *, final_norm):
    x = x_ref[...]
    d = x.shape[1]
    h = _rms(x, g_ref[...]).astype(BF16)
    acc = x
    for c in range(w1_ref.shape[1] // d):
        a = jnp.maximum(jnp.dot(h, w1_ref[:, c * d:(c + 1) * d], preferred_element_type=F32), 0.0)
        acc = acc + jnp.dot((a * a).astype(BF16), w2_ref[c * d:(c + 1) * d, :], preferred_element_type=F32)
    o_ref[...] = _rms(acc, gf_ref[...]) if final_norm else acc


def _ffn(x, gain, w1, w2, gain_final, l, final_norm):
    rows, d = x.shape
    return pl.pallas_call(
        functools.partial(_ffn_kernel, final_norm=final_norm),
        grid=(rows // ROW_TILE,),
        in_specs=[pl.BlockSpec((ROW_TILE, d), lambda i: (i, 0)),
                  _layer(gain, l), _layer(w1, l), _layer(w2, l), _resident(gain_final)],
        out_specs=pl.BlockSpec((ROW_TILE, d), lambda i: (i, 0)),
        out_shape=jax.ShapeDtypeStruct((rows, d), F32),
        compiler_params=_params("parallel"),
        name="ffn",
    )(x, gain, w1, w2, gain_final)


def _alibi_slopes():
    n = 2 * N_HEADS
    return np.asarray(2.0 ** (-ALIBI_MAX_BIAS * np.arange(1, n + 1) / n), np.float32)


def _swa_constants(slopes):
    w = SWA_WINDOW
    s_rel = np.arange(2 * w, dtype=np.float32) - w
    e = np.zeros((2 * w, 2 * LANES), np.float32)
    for o in (HEAD_DIM, LANES):
        e[:, o:o + 3] = s_rel[:, None]
        e[:, o + 3:o + 6] = 1.0
    t_rel = np.arange(w, dtype=np.float32)
    sl = slopes * np.float32(LOG2E)
    aug = np.zeros((N_HEADS, HEAD_DIM, w), np.float32)
    for c, part in enumerate(_split3(sl)):
        aug[:, c, :] = part[:, None]
    for c, part in enumerate(_split3(-sl[:, None] * t_rel[None, :])):
        aug[:, 3 + c, :] = part
    return e.astype(BF16), aug


def _moba_constants(slopes, seq):
    assert seq // MOBA_BLOCK <= MOBA_COLS and (seq // MOBA_BLOCK) % VISIT == 0
    pos = np.arange(seq)
    blk = pos // MOBA_BLOCK
    off = (pos % MOBA_BLOCK).astype(np.float32)
    base = (blk * MOBA_BLOCK).astype(np.float32)
    n = MOBA_COLS
    onehot = (blk[:, None] == np.arange(n)[None, :]).astype(np.float32)
    e = np.zeros((seq, LANES), np.float32)
    for o in (MOBA_HI, 0):
        e[:, o:o + n] = onehot
        e[:, o + n:o + n + 3] = off[:, None]
        e[:, o + n + 3:o + n + 6] = base[:, None]
    aug = np.zeros((N_HEADS, MOBA_COLS, LANES), np.float32)
    for c, part in enumerate(_split3(slopes * np.float32(LOG2E))):
        aug[:, c, :] = part[:, None]
        aug[:, 3 + c, :] = part[:, None]
    return e.astype(BF16), aug


def _ssm_weights(lam_re, lam_im, log_step, b_re, b_im, c_re, c_im):
    lr, li = lam_re.astype(F32), lam_im.astype(F32)
    dt = jnp.exp(log_step.astype(F32))[:, None]
    mag = jnp.exp(lr * dt)
    ab_re, ab_im = mag * jnp.cos(li * dt), mag * jnp.sin(li * dt)
    den = lr * lr + li * li
    nr, ni = ab_re - 1.0, ab_im
    f_re, f_im = (nr * lr + ni * li) / den, (ni * lr - nr * li) / den
    br, bi = b_re.astype(F32), b_im.astype(F32)
    bb_re = f_re[..., None] * br - f_im[..., None] * bi
    bb_im = f_re[..., None] * bi + f_im[..., None] * br
    half = SSM_GROUPS // 2
    row_group = jnp.arange(half * SSM_GROUP) // SSM_GROUP
    lane_group = jnp.arange(half * SSM_STATE) // SSM_STATE

    def in_map(bb):
        rows = jnp.swapaxes(bb, 1, 2).reshape(2, half * SSM_GROUP, SSM_STATE)
        return jnp.where(row_group[:, None] == lane_group[None, :], jnp.tile(rows, (1, 1, half)), 0.0)

    def out_map(c):
        rows = jnp.swapaxes(c, 1, 2).reshape(2, half * SSM_STATE, SSM_GROUP)
        return jnp.where(lane_group[:, None] == row_group[None, :], jnp.tile(rows, (1, 1, half)), 0.0)

    a = jnp.stack([ab_re.reshape(-1), ab_im.reshape(-1)])
    return (in_map(bb_re).astype(BF16), in_map(bb_im).astype(BF16), a,
            out_map(c_re.astype(F32)).astype(BF16), out_map(-c_im.astype(F32)).astype(BF16))


def kernel(x, norm_mix, w_in, sinks, lam_re, lam_im, log_step, b_re, b_im, c_re, c_im, d_skip, w_glu,
           w_o_swa, w_o_ssm, w_o_moba, w_out, norm_ffn, w_ff1, w_ff2, norm_final):
    batch, seq, d = x.shape
    depth = w_in.shape[0]
    assert batch * 2 == SUBLANES, "the S5 scan packs two time steps of all batches into one vreg"
    slopes = _alibi_slopes()
    swa_e, swa_aug = _swa_constants(slopes[:N_HEADS])
    moba_e, moba_aug = _moba_constants(slopes[N_HEADS:], seq)

    w_all = w_in.astype(BF16)
    gain_mix = norm_mix.reshape(depth, 1, d)
    gain_ffn = norm_ffn.reshape(depth, 1, d)
    gain_final = norm_final.reshape(1, d)
    sink_tab = jnp.repeat(sinks.astype(F32) * LOG2E, SWA_WINDOW, axis=1).reshape(depth, 1, N_HEADS * SWA_WINDOW)
    wbr, wbi, a, cr, ci = jax.vmap(_ssm_weights)(lam_re, lam_im, log_step, b_re, b_im, c_re, c_im)
    skip = d_skip.astype(F32).reshape(depth, 1, SSM_WIDTH)
    wglu = w_glu.astype(BF16)
    woa, wos, wom = w_o_swa.astype(BF16), w_o_ssm.astype(BF16), w_o_moba.astype(BF16)
    wout, w1, w2 = w_out.astype(BF16), w_ff1.astype(BF16), w_ff2.astype(BF16)

    xs = x.reshape(batch * seq, d)
    for l in range(depth):
        qa, ka, va, u, qm, km, vm = _norm_proj(xs, gain_mix, w_all, l)
        oa = _swa(qa, ka, va, swa_e, swa_aug, sink_tab, batch, l)
        gl = _ssm(u, wbr, wbi, a, cr, ci, skip, wglu, batch, l)
        om = _moba(qm, km, vm, moba_e, moba_aug, batch)
        xs = _merge(xs, oa, gl, om, gain_mix, w_all, woa, wos, wom, wout, l)
        xs = _ffn(xs, gain_ffn, w1, w2, gain_final, l, final_norm=(l == depth - 1))
    return xs.reshape(batch, seq, d)
```
